```python
import math
import jax, jax.numpy as jnp
from jax import lax
import numpy as np

D_MODEL = 1024
BATCH = 16
SEQ = 2048
DEPTH = 4

CONV_W = D_MODEL // 4
CONV_WIDTH = 31
DIFF_HEADS = 4
DIFF_HD = D_MODEL // 16
DIFF_VD = 2 * DIFF_HD
DN_HEADS = 4
DN_HD = D_MODEL // 16
DN_CONV = 5
DN_CHUNK = 64
Q_BLOCK = 128
ROPE_THETA = 10000.0
MIX_WIDTH = CONV_W + DIFF_HEADS * DIFF_VD + DN_HEADS * DN_HD

D_FF = 11 * D_MODEL // 4
N_EXPERTS = 8
TOP_K = 2
D_FF_EXPERT = 7 * D_MODEL // 2
N_DENSE = (DEPTH + 1) // 2
N_MOE = DEPTH // 2

DEEPNORM_ALPHA = (2 * DEPTH) ** 0.25
DEEPNORM_BETA = (8 * DEPTH) ** -0.25
LN_EPS = 1e-5

PROJ_SIZES = (2 * CONV_W,
              DIFF_HEADS * 2 * DIFF_HD,
              DIFF_HEADS * 2 * DIFF_HD,
              DIFF_HEADS * DIFF_VD,
              3 * DN_HEADS * DN_HD,
              DN_HEADS * DN_HD,
              2 * DN_HEADS,
              2 * DN_HEADS)
PROJ_DIM = sum(PROJ_SIZES)

kernel_name = 'hybrid_conv_diffattn_deltanet_moe_encoder'


def layer_norm(x, g, b):
    xf = x.astype(jnp.float32)
    mu = jnp.mean(xf, axis=-1, keepdims=True)
    var = jnp.mean(jnp.square(xf - mu), axis=-1, keepdims=True)
    return ((xf - mu) * lax.rsqrt(var + LN_EPS) * g.astype(jnp.float32) + b.astype(jnp.float32)).astype(x.dtype)


def rms_norm(x, g):
    xf = x.astype(jnp.float32)
    y = xf * lax.rsqrt(jnp.mean(jnp.square(xf), axis=-1, keepdims=True) + LN_EPS)
    return (y * g.astype(jnp.float32)).astype(x.dtype)


def l2_norm(x):
    return x * lax.rsqrt(jnp.sum(jnp.square(x), axis=-1, keepdims=True) + 1e-6)


def depthwise_conv(x, w):
    width, ch = w.shape
    pad = (width - 1) // 2
    return lax.conv_general_dilated(x, w[:, None, :].astype(x.dtype), window_strides=(1,),
                                    padding=[(pad, pad)], dimension_numbers=('NWC', 'WIO', 'NWC'),
                                    feature_group_count=ch)


def split_projection(proj):
    idx, acc = [], 0
    for s in PROJ_SIZES[:-1]:
        acc += s
        idx.append(acc)
    return jnp.split(proj, idx, axis=-1)


def rope_tables(seq_len, dim):
    inv = ROPE_THETA ** (-jnp.arange(0, dim, 2, dtype=jnp.float32) / dim)
    ang = jnp.arange(seq_len, dtype=jnp.float32)[:, None] * inv[None, :]
    return jnp.cos(ang), jnp.sin(ang)


def apply_rope(x, cos, sin):
    half = x.shape[-1] // 2
    xf = x.astype(jnp.float32)
    c = cos[None, :, None, None, :]
    s = sin[None, :, None, None, :]
    x1, x2 = xf[..., :half], xf[..., half:]
    return jnp.concatenate([x1 * c - x2 * s, x2 * c + x1 * s], axis=-1).astype(x.dtype)


def differential_attention(q, k, v, lam, subln_g, lambda_init):
    B, S, H, _, dh = q.shape
    nb = S // Q_BLOCK
    scale = dh ** -0.5
    q_blocks = jnp.moveaxis(q.reshape(B, nb, Q_BLOCK, H, 2, dh), 1, 0)

    def one_block(qb):
        s = jnp.einsum('bqhcd,bkhcd->bhcqk', qb, k).astype(jnp.float32) * scale
        p = jax.nn.softmax(s, axis=-1)
        a = p[:, :, 0] - lam * p[:, :, 1]
        return jnp.einsum('bhqk,bkhe->bqhe', a.astype(v.dtype), v)

    o = lax.map(one_block, q_blocks)
    o = jnp.moveaxis(o, 0, 1).reshape(B, S, H, -1)
    return rms_norm(o, subln_g) * (1.0 - lambda_init)


def chunk_gated_delta_rule(q, k, v, g, beta):
    B, H, S, dk = q.shape
    dv = v.shape[-1]
    C = DN_CHUNK
    N = S // C
    q = (q * dk ** -0.5).reshape(B, H, N, C, dk)
    k = k.reshape(B, H, N, C, dk)
    v = v.reshape(B, H, N, C, dv)
    beta = beta.reshape(B, H, N, C, 1)
    g = jnp.cumsum(g.reshape(B, H, N, C), axis=-1)
    lower_incl = jnp.tril(jnp.ones((C, C), dtype=bool))
    strict = jnp.tril(jnp.ones((C, C), dtype=bool), -1)
    decay = jnp.exp(jnp.where(lower_incl, g[..., :, None] - g[..., None, :], -jnp.inf))
    kb = k * beta
    kk = jnp.einsum('bhncd,bhnmd->bhncm', kb, k) * decay
    tri = jnp.eye(C, dtype=q.dtype) + jnp.where(strict, kk, 0.0)
    rhs = jnp.concatenate([v * beta, kb * jnp.exp(g)[..., None]], axis=-1)
    sol = lax.linalg.triangular_solve(tri, rhs, left_side=True, lower=True)
    u, w = sol[..., :dv], sol[..., dv:]
    qk = jnp.einsum('bhncd,bhnmd->bhncm', q, k) * decay
    g_last = g[..., -1:]
    q_dec = q * jnp.exp(g)[..., None]
    k_dec = k * jnp.exp(g_last - g)[..., None]
    chunk_decay = jnp.exp(g_last[..., 0])

    def step(state, inp):
        qk_c, u_c, w_c, q_c, k_c, d_c = inp
        v_new = u_c - jnp.einsum('bhcd,bhde->bhce', w_c, state)
        out = jnp.einsum('bhcd,bhde->bhce', q_c, state) + jnp.einsum('bhcm,bhme->bhce', qk_c, v_new)
        state = state * d_c[..., None, None] + jnp.einsum('bhcd,bhce->bhde', k_c, v_new)
        return state, out

    xs = (jnp.moveaxis(qk, 2, 0), jnp.moveaxis(u, 2, 0), jnp.moveaxis(w, 2, 0),
          jnp.moveaxis(q_dec, 2, 0), jnp.moveaxis(k_dec, 2, 0), jnp.moveaxis(chunk_decay, 2, 0))
    state0 = jnp.zeros((B, H, dk, dv), q.dtype)
    _, out = lax.scan(step, state0, xs)
    return jnp.moveaxis(out, 0, 2).reshape(B, H, S, dv)


def gated_deltanet_bidir(p_qkv, p_z, p_b, p_a, conv_w, a_log, dt_bias, norm_g):
    B, S, _ = p_qkv.shape
    qkv = jax.nn.silu(depthwise_conv(p_qkv, conv_w))
    q, k, v = jnp.split(qkv, 3, axis=-1)

    def to_heads(t):
        return t.reshape(B, S, DN_HEADS, DN_HD).astype(jnp.float32).transpose(0, 2, 1, 3)

    q = l2_norm(to_heads(q))
    k = l2_norm(to_heads(k))
    v = to_heads(v)
    beta = jax.nn.sigmoid(p_b.astype(jnp.float32)).reshape(B, S, 2, DN_HEADS).transpose(2, 0, 3, 1)
    a_in = p_a.astype(jnp.float32).reshape(B, S, 2, DN_HEADS).transpose(2, 0, 3, 1)
    g = -jnp.exp(a_log.astype(jnp.float32))[:, None, :, None] * jax.nn.softplus(
        a_in + dt_bias.astype(jnp.float32)[:, None, :, None])
    o_fwd = chunk_gated_delta_rule(q, k, v, g[0], beta[0])
    o_bwd = jnp.flip(chunk_gated_delta_rule(jnp.flip(q, 2), jnp.flip(k, 2), jnp.flip(v, 2),
                                            jnp.flip(g[1], 2), jnp.flip(beta[1], 2)), 2)
    o = (o_fwd + o_bwd).transpose(0, 2, 1, 3)
    z = p_z.reshape(B, S, DN_HEADS, DN_HD).astype(jnp.float32)
    o = rms_norm(o, norm_g) * jax.nn.silu(z)
    return o.reshape(B, S, DN_HEADS * DN_HD).astype(p_qkv.dtype)


def hybrid_mixer(x, cos, sin, lambda_init, w_in, w_o, conv_dw, conv_dw_b, conv_ln_g, conv_ln_b, conv_pw,
                 diff_lambda, diff_subln_g, dn_conv, dn_a_log, dn_dt_bias, dn_norm_g):
    B, S, _ = x.shape
    proj = x @ w_in
    p_conv, p_q, p_k, p_v, p_dn, p_z, p_b, p_a = split_projection(proj)

    c_lin, c_gate = jnp.split(p_conv, 2, axis=-1)
    c = c_lin * jax.nn.sigmoid(c_gate)
    c = depthwise_conv(c, conv_dw) + conv_dw_b
    c = jax.nn.silu(layer_norm(c, conv_ln_g, conv_ln_b))
    y_conv = c @ conv_pw

    q = apply_rope(p_q.reshape(B, S, DIFF_HEADS, 2, DIFF_HD), cos, sin)
    k = apply_rope(p_k.reshape(B, S, DIFF_HEADS, 2, DIFF_HD), cos, sin)
    v = p_v.reshape(B, S, DIFF_HEADS, DIFF_VD)
    lf = diff_lambda.astype(jnp.float32)
    lam = jnp.exp(jnp.sum(lf[0] * lf[1])) - jnp.exp(jnp.sum(lf[2] * lf[3])) + lambda_init
    y_diff = differential_attention(q, k, v, lam, diff_subln_g, lambda_init).reshape(B, S, DIFF_HEADS * DIFF_VD)

    y_dn = gated_deltanet_bidir(p_dn, p_z, p_b, p_a, dn_conv, dn_a_log, dn_dt_bias, dn_norm_g)

    return jnp.concatenate([y_conv, y_diff, y_dn], axis=-1) @ w_o


def swiglu(x, w1, w3, w2):
    return (jax.nn.silu(x @ w1) * (x @ w3)) @ w2


def moe_swiglu(x, router_w, w1, w3, w2):
    B, S, D = x.shape
    t = x.reshape(B * S, D)
    logits = (t @ router_w).astype(jnp.float32)
    top_v, top_i = lax.top_k(logits, TOP_K)
    gates = jax.nn.softmax(top_v, axis=-1)
    combine = jnp.sum(jax.nn.one_hot(top_i, N_EXPERTS, dtype=jnp.float32) * gates[..., None], axis=1)
    out = jnp.zeros_like(t)
    for e in range(N_EXPERTS):
        out = out + combine[:, e:e + 1].astype(t.dtype) * swiglu(t, w1[e], w3[e], w2[e])
    return out.reshape(B, S, D)


def setup_inputs(seed: int = 0) -> dict:
    key = jax.random.key(seed)
    keys = jax.random.split(key, 32)
    counter = iter(range(32))

    def nk():
        return keys[next(counter)]

    def nrm(shape, scale):
        return jax.random.normal(nk(), shape, jnp.float32) * scale

    L = DEPTH
    dt = jnp.exp(jax.random.uniform(nk(), (L, 2, DN_HEADS), jnp.float32, math.log(1e-3), math.log(1e-1)))
    a_log = jnp.log(jax.random.uniform(nk(), (L, 2, DN_HEADS), jnp.float32, 1.0, 16.0))
    return {
        'x': nrm((BATCH, SEQ, D_MODEL), 1.0),
        'w_in': nrm((L, D_MODEL, PROJ_DIM), D_MODEL ** -0.5),
        'w_o': nrm((L, MIX_WIDTH, D_MODEL), MIX_WIDTH ** -0.5 * DEEPNORM_BETA),
        'ln1_g': 1.0 + nrm((L, D_MODEL), 0.02),
        'ln1_b': nrm((L, D_MODEL), 0.02),
        'ln2_g': 1.0 + nrm((L, D_MODEL), 0.02),
        'ln2_b': nrm((L, D_MODEL), 0.02),
        'conv_dw': nrm((L, CONV_WIDTH, CONV_W), CONV_WIDTH ** -0.5),
        'conv_dw_b': nrm((L, CONV_W), 0.02),
        'conv_ln_g': 1.0 + nrm((L, CONV_W), 0.02),
        'conv_ln_b': nrm((L, CONV_W), 0.02),
        'conv_pw': nrm((L, CONV_W, CONV_W), CONV_W ** -0.5),
        'diff_lambda': nrm((L, 4, DIFF_HD), 0.1),
        'diff_subln_g': 1.0 + nrm((L, DIFF_VD), 0.02),
        'dn_conv': nrm((L, DN_CONV, 3 * DN_HEADS * DN_HD), DN_CONV ** -0.5),
        'dn_a_log': a_log,
        'dn_dt_bias': dt + jnp.log(-jnp.expm1(-dt)),
        'dn_norm_g': 1.0 + nrm((L, DN_HD), 0.02),
        'ffn_w1': nrm((N_DENSE, D_MODEL, D_FF), D_MODEL ** -0.5),
        'ffn_w3': nrm((N_DENSE, D_MODEL, D_FF), D_MODEL ** -0.5),
        'ffn_w2': nrm((N_DENSE, D_FF, D_MODEL), D_FF ** -0.5 * DEEPNORM_BETA),
        'router_w': nrm((N_MOE, D_MODEL, N_EXPERTS), D_MODEL ** -0.5),
        'moe_w1': nrm((N_MOE, N_EXPERTS, D_MODEL, D_FF_EXPERT), D_MODEL ** -0.5),
        'moe_w3': nrm((N_MOE, N_EXPERTS, D_MODEL, D_FF_EXPERT), D_MODEL ** -0.5),
        'moe_w2': nrm((N_MOE, N_EXPERTS, D_FF_EXPERT, D_MODEL), D_FF_EXPERT ** -0.5 * DEEPNORM_BETA),
    }


def reference(x, w_in, w_o, ln1_g, ln1_b, ln2_g, ln2_b, conv_dw, conv_dw_b, conv_ln_g, conv_ln_b, conv_pw,
              diff_lambda, diff_subln_g, dn_conv, dn_a_log, dn_dt_bias, dn_norm_g,
              ffn_w1, ffn_w3, ffn_w2, router_w, moe_w1, moe_w3, moe_w2):
    S = x.shape[1]
    cos, sin = rope_tables(S, DIFF_HD)
    for layer in range(DEPTH):
        lambda_init = 0.8 - 0.6 * math.exp(-0.3 * layer)
        h = hybrid_mixer(x, cos, sin, lambda_init, w_in[layer], w_o[layer],
                         conv_dw[layer], conv_dw_b[layer], conv_ln_g[layer], conv_ln_b[layer], conv_pw[layer],
                         diff_lambda[layer], diff_subln_g[layer],
                         dn_conv[layer], dn_a_log[layer], dn_dt_bias[layer], dn_norm_g[layer])
        x = layer_norm(DEEPNORM_ALPHA * x + h, ln1_g[layer], ln1_b[layer])
        j = layer // 2
        if layer % 2 == 0:
            f = swiglu(x, ffn_w1[j], ffn_w3[j], ffn_w2[j])
        else:
            f = moe_swiglu(x, router_w[j], moe_w1[j], moe_w3[j], moe_w2[j])
        x = layer_norm(DEEPNORM_ALPHA * x + f, ln2_g[layer], ln2_b[layer])
    return x
```

```python
import functools
import math

import jax
import jax.numpy as jnp
from jax import lax
from jax.experimental import pallas as pl
from jax.experimental.pallas import tpu as pltpu

F32 = jnp.float32
BF16 = jnp.bfloat16

D_MODEL = 1024
DEPTH = 4
CONV_W = D_MODEL // 4
CONV_WIDTH = 31
DIFF_HEADS = 4
DIFF_HD = D_MODEL // 16
DIFF_VD = 2 * DIFF_HD
DN_HEADS = 4
DN_HD = D_MODEL // 16
DN_CONV = 5
DN_CHUNK = 64
ROPE_THETA = 10000.0
D_FF = 11 * D_MODEL // 4
N_EXPERTS = 8
D_FF_EXPERT = 7 * D_MODEL // 2
DEEPNORM_ALPHA = (2 * DEPTH) ** 0.25
LN_EPS = 1e-5

LANES = 128
SUBLANES = 8
VMEM_LIMIT = 56 * 2 ** 20

COL_CONV = 0
COL_Q = 512
COL_K = 1024
COL_V = 1536
COL_DNQ = 2048
COL_DNK = 2304
COL_DNV = 2560
COL_DNZ = 2816
COL_GATE = 3072
PROJ_RAW = 3088
PROJ_PAD = COL_GATE + 2 * LANES

MOE_TM = 512
MOE_TF = 896


def _cparams(sem):
    return pltpu.CompilerParams(dimension_semantics=sem, vmem_limit_bytes=VMEM_LIMIT)


def _sigmoid(x):
    return 1.0 / (1.0 + jnp.exp(-x))


def _silu(x):
    return x * _sigmoid(x)


def _softplus(x):
    return jnp.maximum(x, 0.0) + jnp.log1p(jnp.exp(-jnp.abs(x)))


def _layer_norm(x, g, b):
    mu = jnp.mean(x, axis=-1, keepdims=True)
    xc = x - mu
    var = jnp.mean(xc * xc, axis=-1, keepdims=True)
    return xc * lax.rsqrt(var + LN_EPS) * g + b


def _dot(a, b):
    return jnp.dot(a, b, preferred_element_type=F32)


def _dot_nt(a, b):
    return lax.dot_general(a, b, (((1,), (1,)), ((), ())), preferred_element_type=F32)


def _dot_tn(a, b):
    return lax.dot_general(a, b, (((0,), (0,)), ((), ())), preferred_element_type=F32)


def _group_sum64(x, ones_bd):
    hi = x.astype(BF16)
    lo = (x - hi.astype(F32)).astype(BF16)
    return _dot(hi, ones_bd) + _dot(lo, ones_bd)


def _ones_blockdiag(n):
    r = lax.broadcasted_iota(jnp.int32, (n, n), 0) // DN_HD
    c = lax.broadcasted_iota(jnp.int32, (n, n), 1) // DN_HD
    return jnp.where(r == c, 1.0, 0.0).astype(BF16)


def _inproj_body(x_ref, w_ref, o_ref):
    o_ref[...] = _dot(x_ref[...].astype(BF16), w_ref[...])


def _inproj(x2, w):
    t, d = x2.shape
    n = w.shape[1]
    tm = min(512, t)
    return pl.pallas_call(
        _inproj_body,
        grid=(t // tm,),
        in_specs=[pl.BlockSpec((tm, d), lambda i: (i, 0)),
                  pl.BlockSpec((d, n), lambda i: (0, 0))],
        out_specs=pl.BlockSpec((tm, n), lambda i: (i, 0)),
        out_shape=jax.ShapeDtypeStruct((t, n), F32),
        name="inproj",
        compiler_params=_cparams(("parallel",)),
    )(x2, w)


CONV_PAD = 16
CONV_ROWS = 128


def _conv_body(p_ref, dw_ref, dwb_ref, g_ref, b_ref, pw_ref, o_ref, pad_scr, *, seq):
    zeros = jnp.zeros((CONV_PAD, CONV_W), F32)
    pad_scr[0:CONV_PAD, :] = zeros
    pad_scr[CONV_PAD + seq:2 * CONV_PAD + seq, :] = zeros

    def glu(i, carry):
        base = pl.multiple_of(i * CONV_ROWS, CONV_ROWS)
        p = p_ref[pl.ds(base, CONV_ROWS), :]
        pad_scr[pl.ds(base + CONV_PAD, CONV_ROWS), :] = p[:, :CONV_W] * _sigmoid(p[:, CONV_W:])
        return carry

    lax.fori_loop(0, seq // CONV_ROWS, glu, 0)
    half = (CONV_WIDTH - 1) // 2

    def conv(i, carry):
        base = pl.multiple_of(i * CONV_ROWS, CONV_ROWS)
        acc = jnp.zeros((CONV_ROWS, CONV_W), F32) + dwb_ref[...]
        win = pad_scr[pl.ds(base, CONV_ROWS + 2 * CONV_PAD), :]
        for j in range(CONV_WIDTH):
            off = CONV_PAD - half + j
            acc = acc + win[off:off + CONV_ROWS, :] * dw_ref[j:j + 1, :]
        y = _silu(_layer_norm(acc, g_ref[...], b_ref[...]))
        o_ref[pl.ds(base, CONV_ROWS), :] = _dot(y.astype(BF16), pw_ref[...])
        return carry

    lax.fori_loop(0, seq // CONV_ROWS, conv, 0)


def _conv_module(proj3, dw, dwb, g, b, pw):
    bsz, seq, _ = proj3.shape
    full = lambda shape: pl.BlockSpec(shape, lambda i: (0,) * len(shape))
    return pl.pallas_call(
        functools.partial(_conv_body, seq=seq),
        grid=(bsz,),
        in_specs=[pl.BlockSpec((None, seq, 2 * CONV_W), lambda i: (i, 0, COL_CONV // (2 * CONV_W))),
                  full((CONV_WIDTH, CONV_W)), full((1, CONV_W)), full((1, CONV_W)), full((1, CONV_W)),
                  full((CONV_W, CONV_W))],
        out_specs=pl.BlockSpec((None, seq, CONV_W), lambda i: (i, 0, 0)),
        out_shape=jax.ShapeDtypeStruct((bsz, seq, CONV_W), F32),
        scratch_shapes=[pltpu.VMEM((seq + 2 * CONV_PAD, CONV_W), F32)],
        name="conv_module",
        compiler_params=_cparams(("parallel",)),
    )(proj3, dw, dwb, g, b, pw)


ATTN_TQ = 256


def _rope(x, cos, sin_signed):
    lane = lax.broadcasted_iota(jnp.int32, x.shape, 1)
    first = (lane % DIFF_HD) < (DIFF_HD // 2)
    rot = jnp.where(first, pltpu.roll(x, LANES - DIFF_HD // 2, 1), pltpu.roll(x, DIFF_HD // 2, 1))
    return x * cos + rot * sin_signed


def _attn_body(q_ref, k_ref, v_ref, cq_ref, sq_ref, ck_ref, sk_ref, dl_ref, g_ref, o_ref, kr_scr, vb_scr,
               *, lambda_init):
    @pl.when(pl.program_id(2) == 0)
    def _():
        kr_scr[...] = _rope(k_ref[...], ck_ref[...], sk_ref[...]).astype(BF16)
        vb_scr[...] = v_ref[...].astype(BF16)

    dl = dl_ref[...]
    lam = (jnp.exp(jnp.sum(dl[0:1] * dl[1:2], axis=-1, keepdims=True))
           - jnp.exp(jnp.sum(dl[2:3] * dl[3:4], axis=-1, keepdims=True)) + lambda_init)

    q = _rope(q_ref[...], cq_ref[...], sq_ref[...]) * (DIFF_HD ** -0.5)
    lane = lax.broadcasted_iota(jnp.int32, q.shape, 1)
    first_map = lane < DIFF_HD
    kr = kr_scr[...]
    vb = vb_scr[...]

    def softmax_pv(qm):
        s = _dot_nt(qm.astype(BF16), kr)
        m = jnp.max(s, axis=-1, keepdims=True)
        p = jnp.exp(s - m)
        l = jnp.sum(p, axis=-1, keepdims=True)
        return _dot(p.astype(BF16), vb) / l

    o = softmax_pv(jnp.where(first_map, q, 0.0)) - lam * softmax_pv(jnp.where(first_map, 0.0, q))
    ms = jnp.mean(o * o, axis=-1, keepdims=True)
    o_ref[...] = o * lax.rsqrt(ms + LN_EPS) * g_ref[...] * (1.0 - lambda_init)


def _diff_attention(proj3, cos_t, sin_t, diff_lambda, subln_g, lambda_init):
    bsz, seq, _ = proj3.shape
    tq = min(ATTN_TQ, seq)
    cq, ck, cv = COL_Q // LANES, COL_K // LANES, COL_V // LANES
    return pl.pallas_call(
        functools.partial(_attn_body, lambda_init=lambda_init),
        grid=(bsz, DIFF_HEADS, seq // tq),
        in_specs=[pl.BlockSpec((None, tq, LANES), lambda b, h, i: (b, i, cq + h)),
                  pl.BlockSpec((None, seq, LANES), lambda b, h, i: (b, 0, ck + h)),
                  pl.BlockSpec((None, seq, LANES), lambda b, h, i: (b, 0, cv + h)),
                  pl.BlockSpec((tq, LANES), lambda b, h, i: (i, 0)),
                  pl.BlockSpec((tq, LANES), lambda b, h, i: (i, 0)),
                  pl.BlockSpec((seq, LANES), lambda b, h, i: (0, 0)),
                  pl.BlockSpec((seq, LANES), lambda b, h, i: (0, 0)),
                  pl.BlockSpec((4, DIFF_HD), lambda b, h, i: (0, 0)),
                  pl.BlockSpec((1, DIFF_VD), lambda b, h, i: (0, 0))],
        out_specs=pl.BlockSpec((None, tq, LANES), lambda b, h, i: (b, i, h)),
        out_shape=jax.ShapeDtypeStruct((bsz, seq, DIFF_HEADS * DIFF_VD), F32),
        scratch_shapes=[pltpu.VMEM((seq, LANES), BF16), pltpu.VMEM((seq, LANES), BF16)],
        name="diff_attention",
        compiler_params=_cparams(("parallel", "parallel", "arbitrary")),
    )(proj3, proj3, proj3, cos_t, sin_t, cos_t, sin_t, diff_lambda, subln_g)


DN_PAD = 8
DN_ROWS = 256
DN_BLK = 16
PAIR = 2 * DN_CHUNK


def _tri_inverse(lmat, same_blk, eye):
    x = jnp.where(same_blk, -lmat, 0.0)
    p = eye + x
    steps = int(math.log2(DN_BLK)) - 1
    for _ in range(steps):
        x = _dot(x, x)
        p = p + _dot(p, x)
    y = -_dot(p, jnp.where(same_blk, 0.0, lmat))
    q = eye + y
    steps = int(math.log2(DN_CHUNK // DN_BLK)) - 1
    for _ in range(steps):
        y = _dot(y, y)
        q = q + _dot(q, y)
    return _dot(q, p)


def _dn_body(q_ref, k_ref, v_ref, z_ref, gt_ref, cwq_ref, cwk_ref, cwv_ref, prm_ref, ng_ref, o_ref,
             pad_scr, q_scr, k_scr, v_scr, g_scr, dec_scr, u_scr, w_scr, qd_scr, kd_scr, qk_scr, *, seq):
    c = DN_CHUNK
    n_chunks = seq // c
    ones_bd = _ones_blockdiag(LANES)
    zeros = jnp.zeros((DN_PAD, LANES), F32)
    pad_scr[0:DN_PAD, :] = zeros
    pad_scr[DN_PAD + seq:2 * DN_PAD + seq, :] = zeros
    half = (DN_CONV - 1) // 2

    def conv_silu(src_ref, cw_ref, dst_scr, normalise):
        def copy(i, carry):
            base = pl.multiple_of(i * DN_ROWS, DN_ROWS)
            pad_scr[pl.ds(base + DN_PAD, DN_ROWS), :] = src_ref[pl.ds(base, DN_ROWS), :]
            return carry

        lax.fori_loop(0, seq // DN_ROWS, copy, 0)

        def conv(i, carry):
            base = pl.multiple_of(i * DN_ROWS, DN_ROWS)
            win = pad_scr[pl.ds(base, DN_ROWS + 2 * DN_PAD), :]
            acc = win[DN_PAD - half:DN_PAD - half + DN_ROWS, :] * cw_ref[0:1, :]
            for j in range(1, DN_CONV):
                off = DN_PAD - half + j
                acc = acc + win[off:off + DN_ROWS, :] * cw_ref[j:j + 1, :]
            y = _silu(acc)
            if normalise:
                y = y * lax.rsqrt(_group_sum64(y * y, ones_bd) + 1e-6)
            dst_scr[pl.ds(base, DN_ROWS), :] = y
            return carry

        lax.fori_loop(0, seq // DN_ROWS, conv, 0)

    conv_silu(q_ref, cwq_ref, q_scr, True)
    conv_silu(k_ref, cwk_ref, k_scr, True)
    conv_silu(v_ref, cwv_ref, v_scr, False)

    a_log = prm_ref[0:1, :]
    dt_bias = prm_ref[1:2, :]

    def gates(i, carry):
        base = pl.multiple_of(i * DN_ROWS, DN_ROWS)
        blk = gt_ref[pl.ds(base, DN_ROWS), :]
        lane = lax.broadcasted_iota(jnp.int32, blk.shape, 1)
        g_scr[pl.ds(base, DN_ROWS), :] = jnp.where(lane < 4, _sigmoid(blk),
                                                   -jnp.exp(a_log) * _softplus(blk + dt_bias))
        return carry

    lax.fori_loop(0, seq // DN_ROWS, gates, 0)

    row2 = lax.broadcasted_iota(jnp.int32, (PAIR, PAIR), 0)
    col2 = lax.broadcasted_iota(jnp.int32, (PAIR, PAIR), 1)
    same_head = (row2 // c) == (col2 // c)
    same_blk = (row2 // DN_BLK) == (col2 // DN_BLK)
    eye = jnp.where(row2 == col2, 1.0, 0.0).astype(F32)
    lane_cl = lax.broadcasted_iota(jnp.int32, (c, LANES), 1)
    row_cl = lax.broadcasted_iota(jnp.int32, (c, LANES), 0)
    head0 = lane_cl < DN_HD

    def stack(x):
        return jnp.concatenate([jnp.where(head0, x, 0.0), jnp.where(head0, 0.0, x)], axis=0)

    def fold(x):
        return x[:c] + x[c:]

    def col_pair(x, lane0):
        return jnp.concatenate([x[:, lane0:lane0 + 1], x[:, lane0 + 1:lane0 + 2]], axis=0)

    def phase1(n, carry):
        rows = pl.ds(pl.multiple_of(n * c, c), c)
        gc = g_scr[rows, :]
        fwd_cum, bwd_cum = gc, gc
        sh = 1
        while sh < c:
            fwd_cum = fwd_cum + jnp.where(row_cl >= sh, pltpu.roll(fwd_cum, sh, 0), 0.0)
            bwd_cum = bwd_cum + jnp.where(row_cl < c - sh, pltpu.roll(bwd_cum, c - sh, 0), 0.0)
            sh *= 2
        cum = jnp.where(lane_cl < 6, fwd_cum, bwd_cum)
        cum_t = cum.T
        tot = jnp.where(lane_cl[0:1] < 6, cum[c - 1:c, :], cum[0:1, :])

        kc = k_scr[rows, :]
        qc = q_scr[rows, :]
        vc = v_scr[rows, :]
        k2 = stack(kc)
        q2 = stack(qc)
        k2b = k2.astype(BF16)
        kk = _dot_nt(k2b, k2b)
        qk = _dot_nt(q2.astype(BF16), k2b)

        for d in range(2):
            beta2 = col_pair(gc, 2 * d)
            cum2 = col_pair(cum, 4 + 2 * d)
            cum_row = jnp.concatenate([cum_t[4 + 2 * d:5 + 2 * d, :], cum_t[5 + 2 * d:6 + 2 * d, :]], axis=1)
            tot2 = jnp.concatenate([jnp.broadcast_to(tot[:, 4 + 2 * d:5 + 2 * d], (c, 1)),
                                    jnp.broadcast_to(tot[:, 5 + 2 * d:6 + 2 * d], (c, 1))], axis=0)
            if d == 0:
                incl, strict = same_head & (row2 >= col2), same_head & (row2 > col2)
            else:
                incl, strict = same_head & (row2 <= col2), same_head & (row2 < col2)
            decay = jnp.exp(jnp.where(incl, cum2 - cum_row, -jnp.inf))
            lmat = jnp.where(strict, beta2 * kk * decay, 0.0)
            tinv = _tri_inverse(lmat, same_blk, eye)
            u = fold(_dot(tinv, stack(vc) * beta2))
            w = fold(_dot(tinv, k2 * (beta2 * jnp.exp(cum2))))
            u_scr[d, rows, :] = u
            w_scr[d, rows, :] = w.astype(BF16)
            qd_scr[d, rows, :] = fold(q2 * ((DN_HD ** -0.5) * jnp.exp(cum2))).astype(BF16)
            kd_scr[d, rows, :] = fold(k2 * jnp.exp(tot2 - cum2)).astype(BF16)
            qk_scr[d, rows, :] = fold(jnp.where(incl, qk * decay, 0.0) * (DN_HD ** -0.5)).astype(BF16)
            dec = jnp.where(lane_cl[0:1] < DN_HD, jnp.exp(tot[:, 4 + 2 * d:5 + 2 * d]),
                            jnp.exp(tot[:, 5 + 2 * d:6 + 2 * d]))
            dec_scr[d, n] = jnp.broadcast_to(dec, (SUBLANES, LANES))
        return carry

    lax.fori_loop(0, n_chunks, phase1, 0)

    o_ref[...] = jnp.zeros((seq, LANES), F32)
    bd_state = (lax.broadcasted_iota(jnp.int32, (LANES, LANES), 0) // DN_HD) == \
               (lax.broadcasted_iota(jnp.int32, (LANES, LANES), 1) // DN_HD)

    def phase2(i, states):
        new_states = []
        for d in range(2):
            n = i if d == 0 else n_chunks - 1 - i
            rows = pl.ds(pl.multiple_of(n * c, c), c)
            state = states[d]
            sb = state.astype(BF16)
            v_new = u_scr[d, rows, :] - _dot(w_scr[d, rows, :], sb)
            vb = v_new.astype(BF16)
            zero = jnp.zeros_like(vb)
            v_st = jnp.concatenate([jnp.where(head0, vb, zero), jnp.where(head0, zero, vb)], axis=0)
            out = _dot(qd_scr[d, rows, :], sb) + _dot(qk_scr[d, rows, :], v_st)
            o_ref[rows, :] = o_ref[rows, :] + out
            upd = _dot_tn(kd_scr[d, rows, :], vb)
            new_states.append(state * dec_scr[d, n][0:1, :] + jnp.where(bd_state, upd, 0.0))
        return tuple(new_states)

    zero_state = jnp.zeros((LANES, LANES), F32)
    lax.fori_loop(0, n_chunks, phase2, (zero_state, zero_state))

    def finish(i, carry):
        base = pl.multiple_of(i * DN_ROWS, DN_ROWS)
        o = o_ref[pl.ds(base, DN_ROWS), :]
        ms = _group_sum64(o * o, ones_bd) * (1.0 / DN_HD)
        o_ref[pl.ds(base, DN_ROWS), :] = o * lax.rsqrt(ms + LN_EPS) * ng_ref[...] * _silu(z_ref[pl.ds(base, DN_ROWS), :])
        return carry

    lax.fori_loop(0, seq // DN_ROWS, finish, 0)


def _deltanet(proj3, dn_conv, prm, norm_g2):
    bsz, seq, _ = proj3.shape
    cq, ck, cv, cz, cg = (COL_DNQ // LANES, COL_DNK // LANES, COL_DNV // LANES, COL_DNZ // LANES, COL_GATE // LANES)
    col = lambda c0: pl.BlockSpec((None, seq, LANES), lambda b, hp: (b, 0, c0 + hp))
    cw = lambda c0: pl.BlockSpec((DN_CONV, LANES), lambda b, hp: (0, c0 + hp))
    n_chunks = seq // DN_CHUNK
    return pl.pallas_call(
        functools.partial(_dn_body, seq=seq),
        grid=(bsz, DN_HEADS // 2),
        in_specs=[col(cq), col(ck), col(cv), col(cz), col(cg), cw(0), cw(2), cw(4),
                  pl.BlockSpec((None, SUBLANES, LANES), lambda b, hp: (hp, 0, 0)),
                  pl.BlockSpec((1, LANES), lambda b, hp: (0, 0))],
        out_specs=pl.BlockSpec((None, seq, LANES), lambda b, hp: (b, 0, hp)),
        out_shape=jax.ShapeDtypeStruct((bsz, seq, DN_HEADS * DN_HD), F32),
        scratch_shapes=[pltpu.VMEM((seq + 2 * DN_PAD, LANES), F32),
                        pltpu.VMEM((seq, LANES), F32), pltpu.VMEM((seq, LANES), F32), pltpu.VMEM((seq, LANES), F32),
                        pltpu.VMEM((seq, LANES), F32),
                        pltpu.VMEM((2, n_chunks, SUBLANES, LANES), F32),
                        pltpu.VMEM((2, seq, LANES), F32),
                        pltpu.VMEM((2, seq, LANES), BF16), pltpu.VMEM((2, seq, LANES), BF16),
                        pltpu.VMEM((2, seq, LANES), BF16), pltpu.VMEM((2, seq, LANES), BF16)],
        name="deltanet",
        compiler_params=_cparams(("parallel", "parallel")),
    )(proj3, proj3, proj3, proj3, proj3, dn_conv, dn_conv, dn_conv, prm, norm_g2)


def _outproj_body(yc_ref, yd_ref, yn_ref, x_ref, w_ref, g_ref, b_ref, o_ref):
    mix = jnp.concatenate([yc_ref[...], yd_ref[...], yn_ref[...]], axis=-1).astype(BF16)
    h = _dot(mix, w_ref[...])
    o_ref[...] = _layer_norm(DEEPNORM_ALPHA * x_ref[...] + h, g_ref[...], b_ref[...])


def _outproj_ln(yc, yd, yn, x2, w, g, b):
    t, d = x2.shape
    tm = min(512, t)
    row = lambda n: pl.BlockSpec((tm, n), lambda i: (i, 0))
    full = lambda shape: pl.BlockSpec(shape, lambda i: (0,) * len(shape))
    return pl.pallas_call(
        _outproj_body,
        grid=(t // tm,),
        in_specs=[row(yc.shape[1]), row(yd.shape[1]), row(yn.shape[1]), row(d), full(w.shape), full((1, d)), full((1, d))],
        out_specs=row(d),
        out_shape=jax.ShapeDtypeStruct((t, d), F32),
        name="outproj_ln",
        compiler_params=_cparams(("parallel",)),
    )(yc, yd, yn, x2, w, g, b)


def _ffn_body(x_ref, w1_ref, w3_ref, w2_ref, g_ref, b_ref, o_ref):
    x = x_ref[...]
    xb = x.astype(BF16)
    h = _silu(_dot(xb, w1_ref[...])) * _dot(xb, w3_ref[...])
    f = _dot(h.astype(BF16), w2_ref[...])
    o_ref[...] = _layer_norm(DEEPNORM_ALPHA * x + f, g_ref[...], b_ref[...])


def _ffn_ln(x2, w1, w3, w2, g, b):
    t, d = x2.shape
    f = w1.shape[1]
    tm = min(512, t)
    row = pl.BlockSpec((tm, d), lambda i: (i, 0))
    once = lambda shape: pl.BlockSpec(shape, lambda i: (0,) * len(shape), pipeline_mode=pl.Buffered(1))
    return pl.pallas_call(
        _ffn_body,
        grid=(t // tm,),
        in_specs=[row, once((d, f)), once((d, f)), once((f, d)), once((1, d)), once((1, d))],
        out_specs=row,
        out_shape=jax.ShapeDtypeStruct((t, d), F32),
        name="ffn_ln",
        compiler_params=_cparams(("parallel",)),
    )(x2, w1, w3, w2, g, b)


ROUTER_TM = 512
META_E1, META_E2, META_R1, META_R2, META_G1, META_G2 = range(6)


def _router_body(x_ref, rw_ref, meta_ref, cnt_ref, run_scr):
    @pl.when(pl.program_id(0) == 0)
    def _():
        run_scr[...] = jnp.zeros_like(run_scr)

    tm = x_ref.shape[0]
    logits = jnp.dot(x_ref[...], rw_ref[...], preferred_element_type=F32, precision=lax.Precision.HIGHEST)
    lane = lax.broadcasted_iota(jnp.int32, logits.shape, 1)
    logits = jnp.where(lane < N_EXPERTS, logits, -jnp.inf)
    m1 = jnp.max(logits, axis=-1, keepdims=True)
    e1 = jnp.min(jnp.where(logits == m1, lane, LANES), axis=-1, keepdims=True)
    rest = jnp.where(lane == e1, -jnp.inf, logits)
    m2 = jnp.max(rest, axis=-1, keepdims=True)
    e2 = jnp.min(jnp.where(rest == m2, lane, LANES), axis=-1, keepdims=True)
    t = jnp.exp(m2 - m1)
    g1 = 1.0 / (1.0 + t)
    g2 = t / (1.0 + t)

    sel = jnp.where((lane == e1) | (lane == e2), 1.0, 0.0)
    r = lax.broadcasted_iota(jnp.int32, (tm, tm), 0)
    c = lax.broadcasted_iota(jnp.int32, (tm, tm), 1)
    strict_lower = jnp.where(r > c, 1.0, 0.0).astype(BF16)
    rank = _dot(strict_lower, sel.astype(BF16)) + run_scr[0:1, :]
    r1 = jnp.sum(jnp.where(lane == e1, rank, 0.0), axis=-1, keepdims=True)
    r2 = jnp.sum(jnp.where(lane == e2, rank, 0.0), axis=-1, keepdims=True)
    run = run_scr[0:1, :] + jnp.sum(sel, axis=0, keepdims=True)
    run_scr[...] = jnp.broadcast_to(run, run_scr.shape)
    cnt_ref[...] = jnp.broadcast_to(run, cnt_ref.shape)

    meta = jnp.where(lane == META_E1, e1.astype(F32), 0.0)
    meta = jnp.where(lane == META_E2, e2.astype(F32), meta)
    meta = jnp.where(lane == META_R1, r1, meta)
    meta = jnp.where(lane == META_R2, r2, meta)
    meta = jnp.where(lane == META_G1, g1, meta)
    meta = jnp.where(lane == META_G2, g2, meta)
    meta_ref[...] = meta


def _router(x2, rw_pad):
    t, d = x2.shape
    tm = min(ROUTER_TM, t)
    return pl.pallas_call(
        _router_body,
        grid=(t // tm,),
        in_specs=[pl.BlockSpec((tm, d), lambda i: (i, 0)), pl.BlockSpec((d, LANES), lambda i: (0, 0))],
        out_specs=[pl.BlockSpec((tm, LANES), lambda i: (i, 0)), pl.BlockSpec((SUBLANES, LANES), lambda i: (0, 0))],
        out_shape=[jax.ShapeDtypeStruct((t, LANES), F32), jax.ShapeDtypeStruct((SUBLANES, LANES), F32)],
        scratch_shapes=[pltpu.VMEM((SUBLANES, LANES), F32)],
        name="router",
        compiler_params=_cparams(("arbitrary",)),
    )(x2, rw_pad)


DISPATCH_TM = 256


def _dispatch_body(dest_ref, x_ref, xs_in_ref, xs_ref, sem):
    del xs_in_ref
    tm = x_ref.shape[0]
    base = pl.program_id(0) * (2 * tm)

    def issue(r, carry):
        src = x_ref.at[pl.ds(r, 1), :]
        pltpu.make_async_copy(src, xs_ref.at[pl.ds(dest_ref[base + 2 * r], 1), :], sem).start()
        pltpu.make_async_copy(src, xs_ref.at[pl.ds(dest_ref[base + 2 * r + 1], 1), :], sem).start()
        return carry

    lax.fori_loop(0, tm, issue, 0)

    def drain(r, carry):
        pltpu.make_async_copy(x_ref.at[pl.ds(0, 1), :], xs_ref.at[pl.ds(0, 1), :], sem).wait()
        pltpu.make_async_copy(x_ref.at[pl.ds(0, 1), :], xs_ref.at[pl.ds(0, 1), :], sem).wait()
        return carry

    lax.fori_loop(0, tm, drain, 0)


def _dispatch(dest, x2, n_slots):
    t, d = x2.shape
    tm = min(DISPATCH_TM, t)
    zeros = jnp.zeros((n_slots, d), F32)
    grid_spec = pltpu.PrefetchScalarGridSpec(
        num_scalar_prefetch=1,
        grid=(t // tm,),
        in_specs=[pl.BlockSpec((tm, d), lambda i, dest: (i, 0)), pl.BlockSpec(memory_space=pl.ANY)],
        out_specs=pl.BlockSpec(memory_space=pl.ANY),
        scratch_shapes=[pltpu.SemaphoreType.DMA],
    )
    return pl.pallas_call(
        _dispatch_body,
        grid_spec=grid_spec,
        out_shape=jax.ShapeDtypeStruct((n_slots, d), F32),
        input_output_aliases={2: 0},
        name="moe_dispatch",
        compiler_params=pltpu.CompilerParams(dimension_semantics=("arbitrary",), vmem_limit_bytes=VMEM_LIMIT,
                                             has_side_effects=True),
    )(dest, x2, zeros)


def _experts_body(te_ref, nu_ref, xs_ref, w1_ref, w3_ref, w2_ref, ys_ref, xb_scr, acc_scr):
    i = pl.program_id(0)
    f = pl.program_id(1)
    used = i < nu_ref[0]

    @pl.when(used & (f == 0))
    def _():
        xb_scr[...] = xs_ref[...].astype(BF16)

    @pl.when(used)
    def _():
        xb = xb_scr[...]
        h = _silu(_dot(xb, w1_ref[...])) * _dot(xb, w3_ref[...])
        part = _dot(h.astype(BF16), w2_ref[...])

        @pl.when(f == 0)
        def _():
            acc_scr[...] = part

        @pl.when(f > 0)
        def _():
            acc_scr[...] = acc_scr[...] + part

    @pl.when(f == pl.num_programs(1) - 1)
    def _():
        @pl.when(used)
        def _():
            ys_ref[...] = acc_scr[...]

        @pl.when(jnp.logical_not(used))
        def _():
            ys_ref[...] = jnp.zeros_like(ys_ref)


def _experts(tile_expert, n_used, xs, w1, w3, w2):
    n_slots, d = xs.shape
    n_tiles = n_slots // MOE_TM
    nf = w1.shape[2] // MOE_TF

    def fidx(i, f, te, nu):
        return jnp.where(i < nu[0], f, nf - 1)

    grid_spec = pltpu.PrefetchScalarGridSpec(
        num_scalar_prefetch=2,
        grid=(n_tiles, nf),
        in_specs=[pl.BlockSpec((MOE_TM, d), lambda i, f, te, nu: (i, 0)),
                  pl.BlockSpec((None, d, MOE_TF), lambda i, f, te, nu: (te[i], 0, fidx(i, f, te, nu))),
                  pl.BlockSpec((None, d, MOE_TF), lambda i, f, te, nu: (te[i], 0, fidx(i, f, te, nu))),
                  pl.BlockSpec((None, MOE_TF, d), lambda i, f, te, nu: (te[i], fidx(i, f, te, nu), 0))],
        out_specs=pl.BlockSpec((MOE_TM, d), lambda i, f, te, nu: (i, 0)),
        scratch_shapes=[pltpu.VMEM((MOE_TM, d), BF16), pltpu.VMEM((MOE_TM, d), F32)],
    )
    return pl.pallas_call(
        _experts_body,
        grid_spec=grid_spec,
        out_shape=jax.ShapeDtypeStruct((n_slots, d), F32),
        name="moe_experts",
        compiler_params=_cparams(("arbitrary", "arbitrary")),
    )(tile_expert, n_used, xs, w1, w3, w2)


COMBINE_TM = 256


def _combine_body(dest_ref, x_ref, meta_ref, ys_ref, g_ref, b_ref, o_ref, buf1, buf2, sem):
    tm = x_ref.shape[0]
    base = pl.program_id(0) * (2 * tm)

    def issue(r, carry):
        pltpu.make_async_copy(ys_ref.at[pl.ds(dest_ref[base + 2 * r], 1), :], buf1.at[pl.ds(r, 1), :], sem).start()
        pltpu.make_async_copy(ys_ref.at[pl.ds(dest_ref[base + 2 * r + 1], 1), :], buf2.at[pl.ds(r, 1), :], sem).start()
        return carry

    lax.fori_loop(0, tm, issue, 0)

    def drain(r, carry):
        pltpu.make_async_copy(ys_ref.at[pl.ds(0, 1), :], buf1.at[pl.ds(0, 1), :], sem).wait()
        pltpu.make_async_copy(ys_ref.at[pl.ds(0, 1), :], buf2.at[pl.ds(0, 1), :], sem).wait()
        return carry

    lax.fori_loop(0, tm, drain, 0)
    meta = meta_ref[...]
    g1 = meta[:, META_G1:META_G1 + 1]
    g2 = meta[:, META_G2:META_G2 + 1]
    f = g1 * buf1[...] + g2 * buf2[...]
    o_ref[...] = _layer_norm(DEEPNORM_ALPHA * x_ref[...] + f, g_ref[...], b_ref[...])


def _combine_ln(dest, x2, meta, ys, g, b):
    t, d = x2.shape
    tm = min(COMBINE_TM, t)
    grid_spec = pltpu.PrefetchScalarGridSpec(
        num_scalar_prefetch=1,
        grid=(t // tm,),
        in_specs=[pl.BlockSpec((tm, d), lambda i, dest: (i, 0)),
                  pl.BlockSpec((tm, LANES), lambda i, dest: (i, 0)),
                  pl.BlockSpec(memory_space=pl.ANY),
                  pl.BlockSpec((1, d), lambda i, dest: (0, 0)),
                  pl.BlockSpec((1, d), lambda i, dest: (0, 0))],
        out_specs=pl.BlockSpec((tm, d), lambda i, dest: (i, 0)),
        scratch_shapes=[pltpu.VMEM((tm, d), F32), pltpu.VMEM((tm, d), F32), pltpu.SemaphoreType.DMA],
    )
    return pl.pallas_call(
        _combine_body,
        grid_spec=grid_spec,
        out_shape=jax.ShapeDtypeStruct((t, d), F32),
        name="moe_combine_ln",
        compiler_params=_cparams(("arbitrary",)),
    )(dest, x2, meta, ys, g, b)


def _moe_ln(x2, rw_pad, w1, w3, w2, g, b):
    t, d = x2.shape
    meta, cnt = _router(x2, rw_pad)
    counts = cnt[0, :N_EXPERTS].astype(jnp.int32)
    padded = ((counts + MOE_TM - 1) // MOE_TM) * MOE_TM
    ends = jnp.cumsum(padded)
    starts = ends - padded
    n_slots = 2 * t + N_EXPERTS * MOE_TM
    n_tiles = n_slots // MOE_TM
    n_used = (ends[-1] // MOE_TM).astype(jnp.int32)
    tile_start = jnp.arange(n_tiles, dtype=jnp.int32) * MOE_TM
    tile_expert = jnp.sum((tile_start[:, None] >= ends[None, :]).astype(jnp.int32), axis=1)
    last_expert = jnp.sum((((n_used - 1) * MOE_TM) >= ends).astype(jnp.int32))
    tile_expert = jnp.where(jnp.arange(n_tiles) < n_used, tile_expert, last_expert).astype(jnp.int32)
    e1 = meta[:, META_E1].astype(jnp.int32)
    e2 = meta[:, META_E2].astype(jnp.int32)
    dest1 = starts[e1] + meta[:, META_R1].astype(jnp.int32)
    dest2 = starts[e2] + meta[:, META_R2].astype(jnp.int32)
    dest = jnp.stack([dest1, dest2], axis=1).reshape(-1).astype(jnp.int32)

    xs = _dispatch(dest, x2, n_slots)
    ys = _experts(tile_expert, n_used.reshape(1), xs, w1, w3, w2)
    return _combine_ln(dest, x2, meta, ys, g, b)


def _rope_tables(seq):
    inv = ROPE_THETA ** (-jnp.arange(0, DIFF_HD, 2, dtype=F32) / DIFF_HD)
    ang = jnp.arange(seq, dtype=F32)[:, None] * inv[None, :]
    cos, sin = jnp.cos(ang), jnp.sin(ang)
    cos_t = jnp.tile(cos, (1, LANES // cos.shape[1]))
    sin_t = jnp.tile(jnp.concatenate([-sin, sin], axis=1), (1, LANES // (2 * sin.shape[1])))
    return cos_t, sin_t


def _gate_columns():
    cols = []
    for hp in range(DN_HEADS // 2):
        blk = [PROJ_RAW - 16 + d * DN_HEADS + 2 * hp + hl for d in range(2) for hl in range(2)]
        blk += [PROJ_RAW - 8 + d * DN_HEADS + 2 * hp + hl for d in range(2) for hl in range(2)]
        cols.append(blk)
    return cols


def _prep_w_in(w_in_l):
    parts = [w_in_l[:, :COL_GATE]]
    for blk in _gate_columns():
        parts.append(w_in_l[:, jnp.array(blk)])
        parts.append(jnp.zeros((w_in_l.shape[0], LANES - len(blk)), w_in_l.dtype))
    return jnp.concatenate(parts, axis=1).astype(BF16)


def _prep_dn_params(a_log_l, dt_bias_l):
    out = []
    for hp in range(DN_HEADS // 2):
        idx = [(d, 2 * hp + hl) for d in range(2) for hl in range(2)]
        a = jnp.stack([a_log_l[d, h] for d, h in idx])
        t = jnp.stack([dt_bias_l[d, h] for d, h in idx])
        blk = jnp.zeros((SUBLANES, LANES), F32)
        blk = blk.at[0, 4:8].set(a).at[1, 4:8].set(t)
        out.append(blk)
    return jnp.stack(out)


def kernel(x, w_in, w_o, ln1_g, ln1_b, ln2_g, ln2_b, conv_dw, conv_dw_b, conv_ln_g, conv_ln_b, conv_pw,
           diff_lambda, diff_subln_g, dn_conv, dn_a_log, dn_dt_bias, dn_norm_g,
           ffn_w1, ffn_w3, ffn_w2, router_w, moe_w1, moe_w3, moe_w2):
    bsz, seq, d = x.shape
    t = bsz * seq
    cos_t, sin_t = _rope_tables(seq)
    x2 = x.reshape(t, d)
    row = lambda v: v.reshape(1, -1)
    for layer in range(DEPTH):
        lambda_init = 0.8 - 0.6 * math.exp(-0.3 * layer)
        proj = _inproj(x2, _prep_w_in(w_in[layer]))
        proj3 = proj.reshape(bsz, seq, PROJ_PAD)
        y_conv = _conv_module(proj3, conv_dw[layer], row(conv_dw_b[layer]), row(conv_ln_g[layer]),
                              row(conv_ln_b[layer]), conv_pw[layer].astype(BF16))
        y_diff = _diff_attention(proj3, cos_t, sin_t, diff_lambda[layer], row(diff_subln_g[layer]), lambda_init)
        y_dn = _deltanet(proj3, dn_conv[layer], _prep_dn_params(dn_a_log[layer], dn_dt_bias[layer]),
                         row(jnp.tile(dn_norm_g[layer], 2)))
        x2 = _outproj_ln(y_conv.reshape(t, -1), y_diff.reshape(t, -1), y_dn.reshape(t, -1), x2,
                         w_o[layer].astype(BF16), row(ln1_g[layer]), row(ln1_b[layer]))
        j = layer // 2
        if layer % 2 == 0:
            x2 = _ffn_ln(x2, ffn_w1[j].astype(BF16), ffn_w3[j].astype(BF16), ffn_w2[j].astype(BF16),
                         row(ln2_g[layer]), row(ln2_b[layer]))
        else:
            rw_pad = jnp.pad(router_w[j], ((0, 0), (0, LANES - N_EXPERTS)))
            x2 = _moe_ln(x2, rw_pad, moe_w1[j].astype(BF16), moe_w3[j].astype(BF16), moe_w2[j].astype(BF16),
                         row(ln2_g[layer]), row(ln2_b[layer]))
    return x2.reshape(bsz, seq, d)
```

```python
import functools
import math

import jax
import jax.numpy as jnp
from jax import lax
from jax.experimental import pallas as pl
from jax.experimental.pallas import tpu as pltpu

F32 = jnp.float32
BF16 = jnp.bfloat16

D_MODEL = 1024
DEPTH = 4
CONV_W = D_MODEL // 4
CONV_WIDTH = 31
DIFF_HEADS = 4
DIFF_HD = D_MODEL // 16
DIFF_VD = 2 * DIFF_HD
DN_HEADS = 4
DN_HD = D_MODEL // 16
DN_CONV = 5
DN_CHUNK = 64
ROPE_THETA = 10000.0
D_FF = 11 * D_MODEL // 4
N_EXPERTS = 8
D_FF_EXPERT = 7 * D_MODEL // 2
DEEPNORM_ALPHA = (2 * DEPTH) ** 0.25
LN_EPS = 1e-5

LANES = 128
SUBLANES = 8
VMEM_LIMIT = 56 * 2 ** 20

COL_CONV = 0
COL_Q = 512
COL_K = 1024
COL_V = 1536
COL_DNQ = 2048
COL_DNK = 2304
COL_DNV = 2560
COL_DNZ = 2816
COL_GATE = 3072
PROJ_RAW = 3088
PROJ_PAD = COL_GATE + 2 * LANES

MOE_TM = 512
MOE_TF = 896


def _cparams(sem):
    return pltpu.CompilerParams(dimension_semantics=sem, vmem_limit_bytes=VMEM_LIMIT)


def _sigmoid(x):
    return 1.0 / (1.0 + jnp.exp(-x))


def _silu(x):
    return x * _sigmoid(x)


def _softplus(x):
    return jnp.maximum(x, 0.0) + jnp.log1p(jnp.exp(-jnp.abs(x)))


def _layer_norm(x, g, b):
    mu = jnp.mean(x, axis=-1, keepdims=True)
    xc = x - mu
    var = jnp.mean(xc * xc, axis=-1, keepdims=True)
    return xc * lax.rsqrt(var + LN_EPS) * g + b


def _dot(a, b):
    return jnp.dot(a, b, preferred_element_type=F32)


def _dot_nt(a, b):
    return lax.dot_general(a, b, (((1,), (1,)), ((), ())), preferred_element_type=F32)


def _dot_tn(a, b):
    return lax.dot_general(a, b, (((0,), (0,)), ((), ())), preferred_element_type=F32)


def _group_sum64(x, ones_bd):
    hi = x.astype(BF16)
    lo = (x - hi.astype(F32)).astype(BF16)
    return _dot(hi, ones_bd) + _dot(lo, ones_bd)


def _ones_blockdiag(n):
    r = lax.broadcasted_iota(jnp.int32, (n, n), 0) // DN_HD
    c = lax.broadcasted_iota(jnp.int32, (n, n), 1) // DN_HD
    return jnp.where(r == c, 1.0, 0.0).astype(BF16)


def _inproj_body(x_ref, w_ref, o_ref):
    o_ref[...] = _dot(x_ref[...].astype(BF16), w_ref[...])


def _inproj(x2, w):
    t, d = x2.shape
    n = w.shape[1]
    tm = min(512, t)
    return pl.pallas_call(
        _inproj_body,
        grid=(t // tm,),
        in_specs=[pl.BlockSpec((tm, d), lambda i: (i, 0)),
                  pl.BlockSpec((d, n), lambda i: (0, 0))],
        out_specs=pl.BlockSpec((tm, n), lambda i: (i, 0)),
        out_shape=jax.ShapeDtypeStruct((t, n), F32),
        name="inproj",
        compiler_params=_cparams(("parallel",)),
    )(x2, w)


CONV_PAD = 16
CONV_ROWS = 128


def _conv_body(p_ref, dw_ref, dwb_ref, g_ref, b_ref, pw_ref, o_ref, pad_scr, *, seq):
    zeros = jnp.zeros((CONV_PAD, CONV_W), F32)
    pad_scr[0:CONV_PAD, :] = zeros
    pad_scr[CONV_PAD + seq:2 * CONV_PAD + seq, :] = zeros

    def glu(i, carry):
        base = pl.multiple_of(i * CONV_ROWS, CONV_ROWS)
        p = p_ref[pl.ds(base, CONV_ROWS), :]
        pad_scr[pl.ds(base + CONV_PAD, CONV_ROWS), :] = p[:, :CONV_W] * _sigmoid(p[:, CONV_W:])
        return carry

    lax.fori_loop(0, seq // CONV_ROWS, glu, 0)
    half = (CONV_WIDTH - 1) // 2

    def conv(i, carry):
        base = pl.multiple_of(i * CONV_ROWS, CONV_ROWS)
        acc = jnp.zeros((CONV_ROWS, CONV_W), F32) + dwb_ref[...]
        win = pad_scr[pl.ds(base, CONV_ROWS + 2 * CONV_PAD), :]
        for j in range(CONV_WIDTH):
            off = CONV_PAD - half + j
            acc = acc + win[off:off + CONV_ROWS, :] * dw_ref[j:j + 1, :]
        y = _silu(_layer_norm(acc, g_ref[...], b_ref[...]))
        o_ref[pl.ds(base, CONV_ROWS), :] = _dot(y.astype(BF16), pw_ref[...])
        return carry

    lax.fori_loop(0, seq // CONV_ROWS, conv, 0)


def _conv_module(proj3, dw, dwb, g, b, pw):
    bsz, seq, _ = proj3.shape
    full = lambda shape: pl.BlockSpec(shape, lambda i: (0,) * len(shape))
    return pl.pallas_call(
        functools.partial(_conv_body, seq=seq),
        grid=(bsz,),
        in_specs=[pl.BlockSpec((None, seq, 2 * CONV_W), lambda i: (i, 0, COL_CONV // (2 * CONV_W))),
                  full((CONV_WIDTH, CONV_W)), full((1, CONV_W)), full((1, CONV_W)), full((1, CONV_W)),
                  full((CONV_W, CONV_W))],
        out_specs=pl.BlockSpec((None, seq, CONV_W), lambda i: (i, 0, 0)),
        out_shape=jax.ShapeDtypeStruct((bsz, seq, CONV_W), F32),
        scratch_shapes=[pltpu.VMEM((seq + 2 * CONV_PAD, CONV_W), F32)],
        name="conv_module",
        compiler_params=_cparams(("parallel",)),
    )(proj3, dw, dwb, g, b, pw)


ATTN_TQ = 512
ATTN_SUB = 256
ATTN_FOLD = 64


def _rope(x, cos, sin_signed):
    lane = lax.broadcasted_iota(jnp.int32, x.shape, 1)
    first = (lane % DIFF_HD) < (DIFF_HD // 2)
    rot = jnp.where(first, pltpu.roll(x, LANES - DIFF_HD // 2, 1), pltpu.roll(x, DIFF_HD // 2, 1))
    return x * cos + rot * sin_signed


def _attn_body(q_ref, k_ref, v_ref, cq_ref, sq_ref, ck_ref, sk_ref, dl_ref, g_ref, o_ref, kr_scr, vt_scr,
               *, lambda_init):
    @pl.when(pl.program_id(2) == 0)
    def _():
        kr_scr[...] = _rope(k_ref[...], ck_ref[...], sk_ref[...]).astype(BF16)
        vt_scr[...] = v_ref[...].T.astype(BF16)

    dl = dl_ref[...]
    lam = (jnp.exp(jnp.sum(dl[0:1] * dl[1:2], axis=-1, keepdims=True))
           - jnp.exp(jnp.sum(dl[2:3] * dl[3:4], axis=-1, keepdims=True)) + lambda_init)

    q = _rope(q_ref[...], cq_ref[...], sq_ref[...]) * (DIFF_HD ** -0.5)
    lane = lax.broadcasted_iota(jnp.int32, q.shape, 1)
    first_map = lane < DIFF_HD
    kr = kr_scr[...]
    vt = vt_scr[...]

    def fold_keys(x, op):
        part = op(x.reshape(x.shape[0] // ATTN_FOLD, ATTN_FOLD, x.shape[1]), axis=0)
        return op(part, axis=0, keepdims=True)

    n_sub = q.shape[0] // ATTN_SUB
    qms = [jnp.where(first_map if mp == 0 else jnp.logical_not(first_map), q, 0.0)[s * ATTN_SUB:(s + 1) * ATTN_SUB]
           for s in range(n_sub) for mp in range(2)]
    sts = [_dot_nt(kr, qm.astype(BF16)) for qm in qms]
    ms_ = [fold_keys(st, jnp.max) for st in sts]
    ps = [jnp.exp(st - m) for st, m in zip(sts, ms_)]
    ls = [fold_keys(p, jnp.sum) for p in ps]
    os_ = [_dot(vt, p.astype(BF16)) / l for p, l in zip(ps, ls)]
    for s in range(n_sub):
        ot = os_[2 * s] - lam * os_[2 * s + 1]
        ms = jnp.mean(ot * ot, axis=0, keepdims=True)
        o_ref[s * ATTN_SUB:(s + 1) * ATTN_SUB, :] = (ot * lax.rsqrt(ms + LN_EPS) * g_ref[...] * (1.0 - lambda_init)).T


def _diff_attention(proj3, cos_t, sin_t, diff_lambda, subln_g, lambda_init):
    bsz, seq, _ = proj3.shape
    tq = min(ATTN_TQ, seq)
    cq, ck, cv = COL_Q // LANES, COL_K // LANES, COL_V // LANES
    return pl.pallas_call(
        functools.partial(_attn_body, lambda_init=lambda_init),
        grid=(bsz, DIFF_HEADS, seq // tq),
        in_specs=[pl.BlockSpec((None, tq, LANES), lambda b, h, i: (b, i, cq + h)),
                  pl.BlockSpec((None, seq, LANES), lambda b, h, i: (b, 0, ck + h)),
                  pl.BlockSpec((None, seq, LANES), lambda b, h, i: (b, 0, cv + h)),
                  pl.BlockSpec((tq, LANES), lambda b, h, i: (i, 0)),
                  pl.BlockSpec((tq, LANES), lambda b, h, i: (i, 0)),
                  pl.BlockSpec((seq, LANES), lambda b, h, i: (0, 0)),
                  pl.BlockSpec((seq, LANES), lambda b, h, i: (0, 0)),
                  pl.BlockSpec((4, DIFF_HD), lambda b, h, i: (0, 0)),
                  pl.BlockSpec((DIFF_VD, 1), lambda b, h, i: (0, 0))],
        out_specs=pl.BlockSpec((None, tq, LANES), lambda b, h, i: (b, i, h)),
        out_shape=jax.ShapeDtypeStruct((bsz, seq, DIFF_HEADS * DIFF_VD), F32),
        scratch_shapes=[pltpu.VMEM((seq, LANES), BF16), pltpu.VMEM((LANES, seq), BF16)],
        name="diff_attention",
        compiler_params=_cparams(("parallel", "parallel", "arbitrary")),
    )(proj3, proj3, proj3, cos_t, sin_t, cos_t, sin_t, diff_lambda, subln_g)


DN_PAD = 8
DN_ROWS = 256
DN_BLK = 16
PAIR = 2 * DN_CHUNK
DN_UNROLL = 4


def _dn_body(q_ref, k_ref, v_ref, z_ref, gt_ref, cwq_ref, cwk_ref, cwv_ref, prm_ref, ng_ref, o_ref,
             pad_scr, q_scr, k_scr, v_scr, g_scr, dec_scr, m_scr, n_scr, qe_scr, oi_scr, *, seq):
    c = DN_CHUNK
    n_chunks = seq // c
    ones_bd = _ones_blockdiag(LANES)
    zeros = jnp.zeros((DN_PAD, LANES), F32)
    pad_scr[0:DN_PAD, :] = zeros
    pad_scr[DN_PAD + seq:2 * DN_PAD + seq, :] = zeros
    half = (DN_CONV - 1) // 2

    def conv_silu(src_ref, cw_ref, dst_scr, normalise):
        def copy(i, carry):
            base = pl.multiple_of(i * DN_ROWS, DN_ROWS)
            pad_scr[pl.ds(base + DN_PAD, DN_ROWS), :] = src_ref[pl.ds(base, DN_ROWS), :]
            return carry

        lax.fori_loop(0, seq // DN_ROWS, copy, 0)

        def conv(i, carry):
            base = pl.multiple_of(i * DN_ROWS, DN_ROWS)
            win = pad_scr[pl.ds(base, DN_ROWS + 2 * DN_PAD), :]
            acc = win[DN_PAD - half:DN_PAD - half + DN_ROWS, :] * cw_ref[0:1, :]
            for j in range(1, DN_CONV):
                off = DN_PAD - half + j
                acc = acc + win[off:off + DN_ROWS, :] * cw_ref[j:j + 1, :]
            y = _silu(acc)
            if normalise:
                y = y * lax.rsqrt(_group_sum64(y * y, ones_bd) + 1e-6)
            dst_scr[pl.ds(base, DN_ROWS), :] = y
            return carry

        lax.fori_loop(0, seq // DN_ROWS, conv, 0)

    conv_silu(q_ref, cwq_ref, q_scr, True)
    conv_silu(k_ref, cwk_ref, k_scr, True)
    conv_silu(v_ref, cwv_ref, v_scr, False)

    a_log = prm_ref[0:1, :]
    dt_bias = prm_ref[1:2, :]

    def gates(i, carry):
        base = pl.multiple_of(i * DN_ROWS, DN_ROWS)
        blk = gt_ref[pl.ds(base, DN_ROWS), :]
        lane = lax.broadcasted_iota(jnp.int32, blk.shape, 1)
        g_scr[pl.ds(base, DN_ROWS), :] = jnp.where(lane < 4, _sigmoid(blk),
                                                   -jnp.exp(a_log) * _softplus(blk + dt_bias))
        return carry

    lax.fori_loop(0, seq // DN_ROWS, gates, 0)

    row2 = lax.broadcasted_iota(jnp.int32, (PAIR, PAIR), 0)
    col2 = lax.broadcasted_iota(jnp.int32, (PAIR, PAIR), 1)
    same_head = (row2 // c) == (col2 // c)
    same_blk = (row2 // DN_BLK) == (col2 // DN_BLK)
    lane_cl = lax.broadcasted_iota(jnp.int32, (c, LANES), 1)
    row_cl = lax.broadcasted_iota(jnp.int32, (c, LANES), 0)
    head0 = lane_cl < DN_HD

    def stack(x):
        return jnp.concatenate([jnp.where(head0, x, 0.0), jnp.where(head0, 0.0, x)], axis=0)

    def fold(x):
        return x[:c] + x[c:]

    def col_pair(x, lane0):
        return jnp.concatenate([x[:, lane0:lane0 + 1], x[:, lane0 + 1:lane0 + 2]], axis=0)

    bd_state = (lax.broadcasted_iota(jnp.int32, (LANES, LANES), 0) // DN_HD) == \
               (lax.broadcasted_iota(jnp.int32, (LANES, LANES), 1) // DN_HD)

    def phase1_chunk(n):
        rows = pl.ds(pl.multiple_of(n * c, c), c)
        gc = g_scr[rows, :]
        fwd_cum, bwd_cum = gc, gc
        sh = 1
        while sh < c:
            fwd_cum = fwd_cum + jnp.where(row_cl >= sh, pltpu.roll(fwd_cum, sh, 0), 0.0)
            bwd_cum = bwd_cum + jnp.where(row_cl < c - sh, pltpu.roll(bwd_cum, c - sh, 0), 0.0)
            sh *= 2
        cum = jnp.where(lane_cl < 6, fwd_cum, bwd_cum)
        cum_t = cum.T
        tot = jnp.where(lane_cl[0:1] < 6, cum[c - 1:c, :], cum[0:1, :])

        kc = k_scr[rows, :]
        qc = q_scr[rows, :]
        vc = v_scr[rows, :]
        k2 = stack(kc)
        q2 = stack(qc)
        v2 = stack(vc)
        k2b = k2.astype(BF16)
        kq = _dot_nt(jnp.concatenate([k2b, q2.astype(BF16)], axis=0), k2b)
        kk = kq[:PAIR]
        qk = kq[PAIR:]
        return dict(n=n, rows=rows, gc=gc, cum=cum, cum_t=cum_t, tot=tot, k2=k2, q2=q2, v2=v2, kk=kk, qk=qk)

    def phase1_chain(ch, d):
        gc, cum, cum_t, tot = ch["gc"], ch["cum"], ch["cum_t"], ch["tot"]
        beta2 = col_pair(gc, 2 * d)
        cum2 = col_pair(cum, 4 + 2 * d)
        cum_row = jnp.concatenate([cum_t[4 + 2 * d:5 + 2 * d, :], cum_t[5 + 2 * d:6 + 2 * d, :]], axis=1)
        tot2 = jnp.concatenate([jnp.broadcast_to(tot[:, 4 + 2 * d:5 + 2 * d], (c, 1)),
                                jnp.broadcast_to(tot[:, 5 + 2 * d:6 + 2 * d], (c, 1))], axis=0)
        if d == 0:
            incl, strict = same_head & (row2 >= col2), same_head & (row2 > col2)
        else:
            incl, strict = same_head & (row2 <= col2), same_head & (row2 < col2)
        decay = jnp.exp(jnp.where(incl, cum2 - cum_row, -jnp.inf))
        lmat = jnp.where(strict, beta2 * ch["kk"] * decay, 0.0)
        rhs = jnp.concatenate([ch["v2"] * beta2, ch["k2"] * (beta2 * jnp.exp(cum2))], axis=1)
        qk_d = jnp.where(incl, ch["qk"] * decay, 0.0) * (DN_HD ** -0.5)
        qd2 = ch["q2"] * ((DN_HD ** -0.5) * jnp.exp(cum2))
        kd = fold(ch["k2"] * jnp.exp(tot2 - cum2))
        dec = jnp.where(lane_cl[0:1] < DN_HD, jnp.exp(tot[:, 4 + 2 * d:5 + 2 * d]),
                        jnp.exp(tot[:, 5 + 2 * d:6 + 2 * d]))
        return dict(d=d, n=ch["n"], rows=ch["rows"], lmat=lmat, rhs=rhs, qk_d=qk_d, qd2=qd2, kd=kd, dec=dec)

    def phase1(i, carry):
        chunks = [phase1_chunk(i * DN_UNROLL + j) for j in range(DN_UNROLL)]
        chains = [phase1_chain(ch, d) for ch in chunks for d in range(2)]
        xs = [jnp.where(same_blk, -t["lmat"], 0.0) for t in chains]
        zs = [jnp.concatenate([jnp.where(same_blk, 0.0, t["lmat"]), t["rhs"]], axis=1) for t in chains]
        bdot = lambda a, b: _dot(a.astype(BF16), b.astype(BF16))
        zs = [z + bdot(x, z) for x, z in zip(xs, zs)]
        for _ in range(int(math.log2(DN_BLK)) - 1):
            xs = [bdot(x, x) for x in xs]
            zs = [z + bdot(x, z) for x, z in zip(xs, zs)]
        ys = [-z[:, :PAIR] for z in zs]
        rs = [z[:, PAIR:] for z in zs]
        rs = [r + bdot(y, r) for y, r in zip(ys, rs)]
        for _ in range(int(math.log2(DN_CHUNK // DN_BLK)) - 1):
            ys = [bdot(y, y) for y in ys]
            rs = [r + bdot(y, r) for y, r in zip(ys, rs)]
        a_s = [_dot(t["qk_d"].astype(BF16), sol.astype(BF16)) for t, sol in zip(chains, rs)]
        mns = [_dot_tn(t["kd"].astype(BF16), fold(sol).astype(BF16)) for t, sol in zip(chains, rs)]
        for t, a, mn in zip(chains, a_s, mns):
            d, n, rows = t["d"], t["n"], t["rows"]
            n_scr[d, n] = jnp.where(bd_state, mn[:, :LANES], 0.0)
            m_scr[d, n] = jnp.where(bd_state, -mn[:, LANES:], 0.0).astype(BF16)
            qe_scr[d, rows, :] = fold(t["qd2"] - a[:, LANES:]).astype(BF16)
            oi_scr[d, rows, :] = fold(a[:, :LANES])
            dec_scr[d, n] = jnp.broadcast_to(t["dec"], (SUBLANES, LANES))
        return carry

    lax.fori_loop(0, n_chunks // DN_UNROLL, phase1, 0)

    o_ref[...] = jnp.zeros((seq, LANES), F32)

    def phase2(i, states):
        new_states = []
        for d in range(2):
            n = i if d == 0 else n_chunks - 1 - i
            rows = pl.ds(pl.multiple_of(n * c, c), c)
            state = states[d]
            sb = state.astype(BF16)
            o_ref[rows, :] = o_ref[rows, :] + oi_scr[d, rows, :] + _dot(qe_scr[d, rows, :], sb)
            new_states.append(state * dec_scr[d, n][0:1, :] + _dot(m_scr[d, n], sb) + n_scr[d, n])
        return tuple(new_states)

    zero_state = jnp.zeros((LANES, LANES), F32)
    lax.fori_loop(0, n_chunks, phase2, (zero_state, zero_state))

    def finish(i, carry):
        base = pl.multiple_of(i * DN_ROWS, DN_ROWS)
        o = o_ref[pl.ds(base, DN_ROWS), :]
        ms = _group_sum64(o * o, ones_bd) * (1.0 / DN_HD)
        o_ref[pl.ds(base, DN_ROWS), :] = o * lax.rsqrt(ms + LN_EPS) * ng_ref[...] * _silu(z_ref[pl.ds(base, DN_ROWS), :])
        return carry

    lax.fori_loop(0, seq // DN_ROWS, finish, 0)


def _deltanet(proj3, dn_conv, prm, norm_g2):
    bsz, seq, _ = proj3.shape
    cq, ck, cv, cz, cg = (COL_DNQ // LANES, COL_DNK // LANES, COL_DNV // LANES, COL_DNZ // LANES, COL_GATE // LANES)
    col = lambda c0: pl.BlockSpec((None, seq, LANES), lambda b, hp: (b, 0, c0 + hp))
    cw = lambda c0: pl.BlockSpec((DN_CONV, LANES), lambda b, hp: (0, c0 + hp))
    n_chunks = seq // DN_CHUNK
    return pl.pallas_call(
        functools.partial(_dn_body, seq=seq),
        grid=(bsz, DN_HEADS // 2),
        in_specs=[col(cq), col(ck), col(cv), col(cz), col(cg), cw(0), cw(2), cw(4),
                  pl.BlockSpec((None, SUBLANES, LANES), lambda b, hp: (hp, 0, 0)),
                  pl.BlockSpec((1, LANES), lambda b, hp: (0, 0))],
        out_specs=pl.BlockSpec((None, seq, LANES), lambda b, hp: (b, 0, hp)),
        out_shape=jax.ShapeDtypeStruct((bsz, seq, DN_HEADS * DN_HD), F32),
        scratch_shapes=[pltpu.VMEM((seq + 2 * DN_PAD, LANES), F32),
                        pltpu.VMEM((seq, LANES), F32), pltpu.VMEM((seq, LANES), F32), pltpu.VMEM((seq, LANES), F32),
                        pltpu.VMEM((seq, LANES), F32),
                        pltpu.VMEM((2, n_chunks, SUBLANES, LANES), F32),
                        pltpu.VMEM((2, n_chunks, LANES, LANES), BF16),
                        pltpu.VMEM((2, n_chunks, LANES, LANES), F32),
                        pltpu.VMEM((2, seq, LANES), BF16),
                        pltpu.VMEM((2, seq, LANES), F32)],
        name="deltanet",
        compiler_params=_cparams(("parallel", "parallel")),
    )(proj3, proj3, proj3, proj3, proj3, dn_conv, dn_conv, dn_conv, prm, norm_g2)


def _outproj_body(yc_ref, yd_ref, yn_ref, x_ref, w_ref, g_ref, b_ref, o_ref):
    mix = jnp.concatenate([yc_ref[...], yd_ref[...], yn_ref[...]], axis=-1).astype(BF16)
    h = _dot(mix, w_ref[...])
    o_ref[...] = _layer_norm(DEEPNORM_ALPHA * x_ref[...] + h, g_ref[...], b_ref[...])


def _outproj_ln(yc, yd, yn, x2, w, g, b):
    t, d = x2.shape
    tm = min(512, t)
    row = lambda n: pl.BlockSpec((tm, n), lambda i: (i, 0))
    full = lambda shape: pl.BlockSpec(shape, lambda i: (0,) * len(shape))
    return pl.pallas_call(
        _outproj_body,
        grid=(t // tm,),
        in_specs=[row(yc.shape[1]), row(yd.shape[1]), row(yn.shape[1]), row(d), full(w.shape), full((1, d)), full((1, d))],
        out_specs=row(d),
        out_shape=jax.ShapeDtypeStruct((t, d), F32),
        name="outproj_ln",
        compiler_params=_cparams(("parallel",)),
    )(yc, yd, yn, x2, w, g, b)


def _ffn_body(x_ref, w1_ref, w3_ref, w2_ref, g_ref, b_ref, o_ref):
    x = x_ref[...]
    xb = x.astype(BF16)
    h = _silu(_dot(xb, w1_ref[...])) * _dot(xb, w3_ref[...])
    f = _dot(h.astype(BF16), w2_ref[...])
    o_ref[...] = _layer_norm(DEEPNORM_ALPHA * x + f, g_ref[...], b_ref[...])


def _ffn_ln(x2, w1, w3, w2, g, b):
    t, d = x2.shape
    f = w1.shape[1]
    tm = min(512, t)
    row = pl.BlockSpec((tm, d), lambda i: (i, 0))
    once = lambda shape: pl.BlockSpec(shape, lambda i: (0,) * len(shape), pipeline_mode=pl.Buffered(1))
    return pl.pallas_call(
        _ffn_body,
        grid=(t // tm,),
        in_specs=[row, once((d, f)), once((d, f)), once((f, d)), once((1, d)), once((1, d))],
        out_specs=row,
        out_shape=jax.ShapeDtypeStruct((t, d), F32),
        name="ffn_ln",
        compiler_params=_cparams(("parallel",)),
    )(x2, w1, w3, w2, g, b)


ROUTER_TM = 512
META_E1, META_E2, META_R1, META_R2, META_G1, META_G2 = range(6)


def _router_body(x_ref, rw_ref, meta_ref, cnt_ref, run_scr):
    @pl.when(pl.program_id(0) == 0)
    def _():
        run_scr[...] = jnp.zeros_like(run_scr)

    tm = x_ref.shape[0]
    logits = jnp.dot(x_ref[...], rw_ref[...], preferred_element_type=F32, precision=lax.Precision.HIGHEST)
    lane = lax.broadcasted_iota(jnp.int32, logits.shape, 1)
    logits = jnp.where(lane < N_EXPERTS, logits, -jnp.inf)
    m1 = jnp.max(logits, axis=-1, keepdims=True)
    e1 = jnp.min(jnp.where(logits == m1, lane, LANES), axis=-1, keepdims=True)
    rest = jnp.where(lane == e1, -jnp.inf, logits)
    m2 = jnp.max(rest, axis=-1, keepdims=True)
    e2 = jnp.min(jnp.where(rest == m2, lane, LANES), axis=-1, keepdims=True)
    t = jnp.exp(m2 - m1)
    g1 = 1.0 / (1.0 + t)
    g2 = t / (1.0 + t)

    sel = jnp.where((lane == e1) | (lane == e2), 1.0, 0.0)
    r = lax.broadcasted_iota(jnp.int32, (tm, tm), 0)
    c = lax.broadcasted_iota(jnp.int32, (tm, tm), 1)
    strict_lower = jnp.where(r > c, 1.0, 0.0).astype(BF16)
    rank = _dot(strict_lower, sel.astype(BF16)) + run_scr[0:1, :]
    r1 = jnp.sum(jnp.where(lane == e1, rank, 0.0), axis=-1, keepdims=True)
    r2 = jnp.sum(jnp.where(lane == e2, rank, 0.0), axis=-1, keepdims=True)
    run = run_scr[0:1, :] + jnp.sum(sel, axis=0, keepdims=True)
    run_scr[...] = jnp.broadcast_to(run, run_scr.shape)
    cnt_ref[...] = jnp.broadcast_to(run, cnt_ref.shape)

    meta = jnp.where(lane == META_E1, e1.astype(F32), 0.0)
    meta = jnp.where(lane == META_E2, e2.astype(F32), meta)
    meta = jnp.where(lane == META_R1, r1, meta)
    meta = jnp.where(lane == META_R2, r2, meta)
    meta = jnp.where(lane == META_G1, g1, meta)
    meta = jnp.where(lane == META_G2, g2, meta)
    meta_ref[...] = meta


def _router(x2, rw_pad):
    t, d = x2.shape
    tm = min(ROUTER_TM, t)
    return pl.pallas_call(
        _router_body,
        grid=(t // tm,),
        in_specs=[pl.BlockSpec((tm, d), lambda i: (i, 0)), pl.BlockSpec((d, LANES), lambda i: (0, 0))],
        out_specs=[pl.BlockSpec((tm, LANES), lambda i: (i, 0)), pl.BlockSpec((SUBLANES, LANES), lambda i: (0, 0))],
        out_shape=[jax.ShapeDtypeStruct((t, LANES), F32), jax.ShapeDtypeStruct((SUBLANES, LANES), F32)],
        scratch_shapes=[pltpu.VMEM((SUBLANES, LANES), F32)],
        name="router",
        compiler_params=_cparams(("arbitrary",)),
    )(x2, rw_pad)


DISPATCH_TM = 256


def _dispatch_body(dest_ref, x_ref, xs_in_ref, xs_ref, sem):
    del xs_in_ref
    tm = x_ref.shape[0]
    base = pl.program_id(0) * (2 * tm)

    def issue(r, carry):
        src = x_ref.at[pl.ds(r, 1), :]
        pltpu.make_async_copy(src, xs_ref.at[pl.ds(dest_ref[base + 2 * r], 1), :], sem).start()
        pltpu.make_async_copy(src, xs_ref.at[pl.ds(dest_ref[base + 2 * r + 1], 1), :], sem).start()
        return carry

    lax.fori_loop(0, tm, issue, 0)

    def drain(r, carry):
        pltpu.make_async_copy(x_ref.at[pl.ds(0, 1), :], xs_ref.at[pl.ds(0, 1), :], sem).wait()
        pltpu.make_async_copy(x_ref.at[pl.ds(0, 1), :], xs_ref.at[pl.ds(0, 1), :], sem).wait()
        return carry

    lax.fori_loop(0, tm, drain, 0)


def _dispatch(dest, x2, n_slots):
    t, d = x2.shape
    tm = min(DISPATCH_TM, t)
    zeros = jnp.zeros((n_slots, d), F32)
    grid_spec = pltpu.PrefetchScalarGridSpec(
        num_scalar_prefetch=1,
        grid=(t // tm,),
        in_specs=[pl.BlockSpec((tm, d), lambda i, dest: (i, 0)), pl.BlockSpec(memory_space=pl.ANY)],
        out_specs=pl.BlockSpec(memory_space=pl.ANY),
        scratch_shapes=[pltpu.SemaphoreType.DMA],
    )
    return pl.pallas_call(
        _dispatch_body,
        grid_spec=grid_spec,
        out_shape=jax.ShapeDtypeStruct((n_slots, d), F32),
        input_output_aliases={2: 0},
        name="moe_dispatch",
        compiler_params=pltpu.CompilerParams(dimension_semantics=("arbitrary",), vmem_limit_bytes=VMEM_LIMIT,
                                             has_side_effects=True),
    )(dest, x2, zeros)


def _experts_body(te_ref, nu_ref, xs_ref, w1_ref, w3_ref, w2_ref, ys_ref, xb_scr, acc_scr):
    i = pl.program_id(0)
    f = pl.program_id(1)
    used = i < nu_ref[0]

    @pl.when(used & (f == 0))
    def _():
        xb_scr[...] = xs_ref[...].astype(BF16)

    @pl.when(used)
    def _():
        xb = xb_scr[...]
        h = _silu(_dot(xb, w1_ref[...])) * _dot(xb, w3_ref[...])
        part = _dot(h.astype(BF16), w2_ref[...])

        @pl.when(f == 0)
        def _():
            acc_scr[...] = part

        @pl.when(f > 0)
        def _():
            acc_scr[...] = acc_scr[...] + part

    @pl.when(f == pl.num_programs(1) - 1)
    def _():
        @pl.when(used)
        def _():
            ys_ref[...] = acc_scr[...]

        @pl.when(jnp.logical_not(used))
        def _():
            ys_ref[...] = jnp.zeros_like(ys_ref)


def _experts(tile_expert, n_used, xs, w1, w3, w2):
    n_slots, d = xs.shape
    n_tiles = n_slots // MOE_TM
    nf = w1.shape[2] // MOE_TF

    def fidx(i, f, te, nu):
        return jnp.where(i < nu[0], f, nf - 1)

    grid_spec = pltpu.PrefetchScalarGridSpec(
        num_scalar_prefetch=2,
        grid=(n_tiles, nf),
        in_specs=[pl.BlockSpec((MOE_TM, d), lambda i, f, te, nu: (i, 0)),
                  pl.BlockSpec((None, d, MOE_TF), lambda i, f, te, nu: (te[i], 0, fidx(i, f, te, nu))),
                  pl.BlockSpec((None, d, MOE_TF), lambda i, f, te, nu: (te[i], 0, fidx(i, f, te, nu))),
                  pl.BlockSpec((None, MOE_TF, d), lambda i, f, te, nu: (te[i], fidx(i, f, te, nu), 0))],
        out_specs=pl.BlockSpec((MOE_TM, d), lambda i, f, te, nu: (i, 0)),
        scratch_shapes=[pltpu.VMEM((MOE_TM, d), BF16), pltpu.VMEM((MOE_TM, d), F32)],
    )
    return pl.pallas_call(
        _experts_body,
        grid_spec=grid_spec,
        out_shape=jax.ShapeDtypeStruct((n_slots, d), F32),
        name="moe_experts",
        compiler_params=_cparams(("arbitrary", "arbitrary")),
    )(tile_expert, n_used, xs, w1, w3, w2)


COMBINE_TM = 256


def _combine_body(dest_ref, x_ref, meta_ref, ys_ref, g_ref, b_ref, o_ref, buf1, buf2, sem):
    tm = x_ref.shape[0]
    base = pl.program_id(0) * (2 * tm)

    def issue(r, carry):
        pltpu.make_async_copy(ys_ref.at[pl.ds(dest_ref[base + 2 * r], 1), :], buf1.at[pl.ds(r, 1), :], sem).start()
        pltpu.make_async_copy(ys_ref.at[pl.ds(dest_ref[base + 2 * r + 1], 1), :], buf2.at[pl.ds(r, 1), :], sem).start()
        return carry

    lax.fori_loop(0, tm, issue, 0)

    def drain(r, carry):
        pltpu.make_async_copy(ys_ref.at[pl.ds(0, 1), :], buf1.at[pl.ds(0, 1), :], sem).wait()
        pltpu.make_async_copy(ys_ref.at[pl.ds(0, 1), :], buf2.at[pl.ds(0, 1), :], sem).wait()
        return carry

    lax.fori_loop(0, tm, drain, 0)
    meta = meta_ref[...]
    g1 = meta[:, META_G1:META_G1 + 1]
    g2 = meta[:, META_G2:META_G2 + 1]
    f = g1 * buf1[...] + g2 * buf2[...]
    o_ref[...] = _layer_norm(DEEPNORM_ALPHA * x_ref[...] + f, g_ref[...], b_ref[...])


def _combine_ln(dest, x2, meta, ys, g, b):
    t, d = x2.shape
    tm = min(COMBINE_TM, t)
    grid_spec = pltpu.PrefetchScalarGridSpec(
        num_scalar_prefetch=1,
        grid=(t // tm,),
        in_specs=[pl.BlockSpec((tm, d), lambda i, dest: (i, 0)),
                  pl.BlockSpec((tm, LANES), lambda i, dest: (i, 0)),
                  pl.BlockSpec(memory_space=pl.ANY),
                  pl.BlockSpec((1, d), lambda i, dest: (0, 0)),
                  pl.BlockSpec((1, d), lambda i, dest: (0, 0))],
        out_specs=pl.BlockSpec((tm, d), lambda i, dest: (i, 0)),
        scratch_shapes=[pltpu.VMEM((tm, d), F32), pltpu.VMEM((tm, d), F32), pltpu.SemaphoreType.DMA],
    )
    return pl.pallas_call(
        _combine_body,
        grid_spec=grid_spec,
        out_shape=jax.ShapeDtypeStruct((t, d), F32),
        name="moe_combine_ln",
        compiler_params=_cparams(("arbitrary",)),
    )(dest, x2, meta, ys, g, b)


def _moe_ln(x2, rw_pad, w1, w3, w2, g, b):
    t, d = x2.shape
    meta, cnt = _router(x2, rw_pad)
    counts = cnt[0, :N_EXPERTS].astype(jnp.int32)
    padded = ((counts + MOE_TM - 1) // MOE_TM) * MOE_TM
    ends = jnp.cumsum(padded)
    starts = ends - padded
    n_slots = 2 * t + N_EXPERTS * MOE_TM
    n_tiles = n_slots // MOE_TM
    n_used = (ends[-1] // MOE_TM).astype(jnp.int32)
    tile_start = jnp.arange(n_tiles, dtype=jnp.int32) * MOE_TM
    tile_expert = jnp.sum((tile_start[:, None] >= ends[None, :]).astype(jnp.int32), axis=1)
    last_expert = jnp.sum((((n_used - 1) * MOE_TM) >= ends).astype(jnp.int32))
    tile_expert = jnp.where(jnp.arange(n_tiles) < n_used, tile_expert, last_expert).astype(jnp.int32)
    e1 = meta[:, META_E1].astype(jnp.int32)
    e2 = meta[:, META_E2].astype(jnp.int32)
    dest1 = starts[e1] + meta[:, META_R1].astype(jnp.int32)
    dest2 = starts[e2] + meta[:, META_R2].astype(jnp.int32)
    dest = jnp.stack([dest1, dest2], axis=1).reshape(-1).astype(jnp.int32)

    xs = _dispatch(dest, x2, n_slots)
    ys = _experts(tile_expert, n_used.reshape(1), xs, w1, w3, w2)
    return _combine_ln(dest, x2, meta, ys, g, b)


def _rope_tables(seq):
    inv = ROPE_THETA ** (-jnp.arange(0, DIFF_HD, 2, dtype=F32) / DIFF_HD)
    ang = jnp.arange(seq, dtype=F32)[:, None] * inv[None, :]
    cos, sin = jnp.cos(ang), jnp.sin(ang)
    cos_t = jnp.tile(cos, (1, LANES // cos.shape[1]))
    sin_t = jnp.tile(jnp.concatenate([-sin, sin], axis=1), (1, LANES // (2 * sin.shape[1])))
    return cos_t, sin_t


def _gate_columns():
    cols = []
    for hp in range(DN_HEADS // 2):
        blk = [PROJ_RAW - 16 + d * DN_HEADS + 2 * hp + hl for d in range(2) for hl in range(2)]
        blk += [PROJ_RAW - 8 + d * DN_HEADS + 2 * hp + hl for d in range(2) for hl in range(2)]
        cols.append(blk)
    return cols


def _prep_w_in(w_in_l):
    parts = [w_in_l[:, :COL_GATE]]
    for blk in _gate_columns():
        parts.append(w_in_l[:, jnp.array(blk)])
        parts.append(jnp.zeros((w_in_l.shape[0], LANES - len(blk)), w_in_l.dtype))
    return jnp.concatenate(parts, axis=1).astype(BF16)


def _prep_dn_params(a_log_l, dt_bias_l):
    out = []
    for hp in range(DN_HEADS // 2):
        idx = [(d, 2 * hp + hl) for d in range(2) for hl in range(2)]
        a = jnp.stack([a_log_l[d, h] for d, h in idx])
        t = jnp.stack([dt_bias_l[d, h] for d, h in idx])
        blk = jnp.zeros((SUBLANES, LANES), F32)
        blk = blk.at[0, 4:8].set(a).at[1, 4:8].set(t)
        out.append(blk)
    return jnp.stack(out)


def kernel(x, w_in, w_o, ln1_g, ln1_b, ln2_g, ln2_b, conv_dw, conv_dw_b, conv_ln_g, conv_ln_b, conv_pw,
           diff_lambda, diff_subln_g, dn_conv, dn_a_log, dn_dt_bias, dn_norm_g,
           ffn_w1, ffn_w3, ffn_w2, router_w, moe_w1, moe_w3, moe_w2):
    bsz, seq, d = x.shape
    t = bsz * seq
    cos_t, sin_t = _rope_tables(seq)
    x2 = x.reshape(t, d)
    row = lambda v: v.reshape(1, -1)
    for layer in range(DEPTH):
        lambda_init = 0.8 - 0.6 * math.exp(-0.3 * layer)
        proj = _inproj(x2, _prep_w_in(w_in[layer]))
        proj3 = proj.reshape(bsz, seq, PROJ_PAD)
        y_conv = _conv_module(proj3, conv_dw[layer], row(conv_dw_b[layer]), row(conv_ln_g[layer]),
                              row(conv_ln_b[layer]), conv_pw[layer].astype(BF16))
        y_diff = _diff_attention(proj3, cos_t, sin_t, diff_lambda[layer], diff_subln_g[layer].reshape(-1, 1),
                                 lambda_init)
        y_dn = _deltanet(proj3, dn_conv[layer], _prep_dn_params(dn_a_log[layer], dn_dt_bias[layer]),
                         row(jnp.tile(dn_norm_g[layer], 2)))
        x2 = _outproj_ln(y_conv.reshape(t, -1), y_diff.reshape(t, -1), y_dn.reshape(t, -1), x2,
                         w_o[layer].astype(BF16), row(ln1_g[layer]), row(ln1_b[layer]))
        j = layer // 2
        if layer % 2 == 0:
            x2 = _ffn_ln(x2, ffn_w1[j].astype(BF16), ffn_w3[j].astype(BF16), ffn_w2[j].astype(BF16),
                         row(ln2_g[layer]), row(ln2_b[layer]))
        else:
            rw_pad = jnp.pad(router_w[j], ((0, 0), (0, LANES - N_EXPERTS)))
            x2 = _moe_ln(x2, rw_pad, moe_w1[j].astype(BF16), moe_w3[j].astype(BF16), moe_w2[j].astype(BF16),
                         row(ln2_g[layer]), row(ln2_b[layer]))
    return x2.reshape(bsz, seq, d)
```

```python
import functools
import math

import jax
import jax.numpy as jnp
from jax import lax
from jax.experimental import pallas as pl
from jax.experimental.pallas import tpu as pltpu

F32 = jnp.float32
BF16 = jnp.bfloat16

D_MODEL = 1024
DEPTH = 4
CONV_W = D_MODEL // 4
CONV_WIDTH = 31
DIFF_HEADS = 4
DIFF_HD = D_MODEL // 16
DIFF_VD = 2 * DIFF_HD
DN_HEADS = 4
DN_HD = D_MODEL // 16
DN_CONV = 5
DN_CHUNK = 64
ROPE_THETA = 10000.0
D_FF = 11 * D_MODEL // 4
N_EXPERTS = 8
D_FF_EXPERT = 7 * D_MODEL // 2
DEEPNORM_ALPHA = (2 * DEPTH) ** 0.25
LN_EPS = 1e-5

LANES = 128
SUBLANES = 8
VMEM_LIMIT = 56 * 2 ** 20

COL_CONV = 0
COL_Q = 512
COL_K = 1024
COL_V = 1536
COL_DNQ = 2048
COL_DNK = 2304
COL_DNV = 2560
COL_DNZ = 2816
COL_GATE = 3072
PROJ_RAW = 3088
PROJ_PAD = COL_GATE + 2 * LANES

MOE_TM = 512
MOE_TF = 896


def _cparams(sem):
    return pltpu.CompilerParams(dimension_semantics=sem, vmem_limit_bytes=VMEM_LIMIT)


def _sigmoid(x):
    return 1.0 / (1.0 + jnp.exp(-x))


def _silu(x):
    return x * _sigmoid(x)


def _softplus(x):
    return jnp.maximum(x, 0.0) + jnp.log1p(jnp.exp(-jnp.abs(x)))


def _layer_norm(x, g, b):
    mu = jnp.mean(x, axis=-1, keepdims=True)
    xc = x - mu
    var = jnp.mean(xc * xc, axis=-1, keepdims=True)
    return xc * lax.rsqrt(var + LN_EPS) * g + b


def _dot(a, b):
    return jnp.dot(a, b, preferred_element_type=F32)


def _dot_nt(a, b):
    return lax.dot_general(a, b, (((1,), (1,)), ((), ())), preferred_element_type=F32)


def _dot_tn(a, b):
    return lax.dot_general(a, b, (((0,), (0,)), ((), ())), preferred_element_type=F32)


def _group_sum64(x, ones_bd):
    hi = x.astype(BF16)
    lo = (x - hi.astype(F32)).astype(BF16)
    return _dot(hi, ones_bd) + _dot(lo, ones_bd)


def _ones_blockdiag(n):
    r = lax.broadcasted_iota(jnp.int32, (n, n), 0) // DN_HD
    c = lax.broadcasted_iota(jnp.int32, (n, n), 1) // DN_HD
    return jnp.where(r == c, 1.0, 0.0).astype(BF16)


def _inproj_body(x_ref, w_ref, o_ref):
    o_ref[...] = _dot(x_ref[...].astype(BF16), w_ref[...])


def _inproj(x2, w):
    t, d = x2.shape
    n = w.shape[1]
    tm = min(512, t)
    return pl.pallas_call(
        _inproj_body,
        grid=(t // tm,),
        in_specs=[pl.BlockSpec((tm, d), lambda i: (i, 0)),
                  pl.BlockSpec((d, n), lambda i: (0, 0))],
        out_specs=pl.BlockSpec((tm, n), lambda i: (i, 0)),
        out_shape=jax.ShapeDtypeStruct((t, n), F32),
        name="inproj",
        compiler_params=_cparams(("parallel",)),
    )(x2, w)


CONV_PAD = 16
CONV_ROWS = 128


def _conv_body(p_ref, dw_ref, dwb_ref, g_ref, b_ref, pw_ref, o_ref, pad_scr, *, seq):
    zeros = jnp.zeros((CONV_PAD, CONV_W), F32)
    pad_scr[0:CONV_PAD, :] = zeros
    pad_scr[CONV_PAD + seq:2 * CONV_PAD + seq, :] = zeros

    def glu(i, carry):
        base = pl.multiple_of(i * CONV_ROWS, CONV_ROWS)
        p = p_ref[pl.ds(base, CONV_ROWS), :]
        pad_scr[pl.ds(base + CONV_PAD, CONV_ROWS), :] = p[:, :CONV_W] * _sigmoid(p[:, CONV_W:])
        return carry

    lax.fori_loop(0, seq // CONV_ROWS, glu, 0)
    half = (CONV_WIDTH - 1) // 2

    def conv(i, carry):
        base = pl.multiple_of(i * CONV_ROWS, CONV_ROWS)
        acc = jnp.zeros((CONV_ROWS, CONV_W), F32) + dwb_ref[...]
        win = pad_scr[pl.ds(base, CONV_ROWS + 2 * CONV_PAD), :]
        n_win = CONV_ROWS + 2 * CONV_PAD
        for r in range(SUBLANES):
            wr = win if r == 0 else pltpu.roll(win, n_win - r, 0)
            for j in range(CONV_WIDTH):
                off = CONV_PAD - half + j
                if off % SUBLANES == r:
                    acc = acc + wr[off - r:off - r + CONV_ROWS, :] * dw_ref[j:j + 1, :]
        y = _silu(_layer_norm(acc, g_ref[...], b_ref[...]))
        o_ref[pl.ds(base, CONV_ROWS), :] = _dot(y.astype(BF16), pw_ref[...])
        return carry

    lax.fori_loop(0, seq // CONV_ROWS, conv, 0)


def _conv_module(proj3, dw, dwb, g, b, pw):
    bsz, seq, _ = proj3.shape
    full = lambda shape: pl.BlockSpec(shape, lambda i: (0,) * len(shape))
    return pl.pallas_call(
        functools.partial(_conv_body, seq=seq),
        grid=(bsz,),
        in_specs=[pl.BlockSpec((None, seq, 2 * CONV_W), lambda i: (i, 0, COL_CONV // (2 * CONV_W))),
                  full((CONV_WIDTH, CONV_W)), full((1, CONV_W)), full((1, CONV_W)), full((1, CONV_W)),
                  full((CONV_W, CONV_W))],
        out_specs=pl.BlockSpec((None, seq, CONV_W), lambda i: (i, 0, 0)),
        out_shape=jax.ShapeDtypeStruct((bsz, seq, CONV_W), F32),
        scratch_shapes=[pltpu.VMEM((seq + 2 * CONV_PAD, CONV_W), F32)],
        name="conv_module",
        compiler_params=_cparams(("parallel",)),
    )(proj3, dw, dwb, g, b, pw)


ATTN_TQ = 512
ATTN_SUB = 256
ATTN_FOLD = 64


def _rope(x, cos, sin_signed):
    lane = lax.broadcasted_iota(jnp.int32, x.shape, 1)
    first = (lane % DIFF_HD) < (DIFF_HD // 2)
    rot = jnp.where(first, pltpu.roll(x, LANES - DIFF_HD // 2, 1), pltpu.roll(x, DIFF_HD // 2, 1))
    return x * cos + rot * sin_signed


def _attn_body(q_ref, k_ref, v_ref, cq_ref, sq_ref, ck_ref, sk_ref, dl_ref, g_ref, o_ref, kr_scr, vt_scr,
               *, lambda_init):
    @pl.when(pl.program_id(2) == 0)
    def _():
        kr_scr[...] = _rope(k_ref[...], ck_ref[...], sk_ref[...]).astype(BF16)
        vt_scr[...] = v_ref[...].T.astype(BF16)

    dl = dl_ref[...]
    lam = (jnp.exp(jnp.sum(dl[0:1] * dl[1:2], axis=-1, keepdims=True))
           - jnp.exp(jnp.sum(dl[2:3] * dl[3:4], axis=-1, keepdims=True)) + lambda_init)

    q = _rope(q_ref[...], cq_ref[...], sq_ref[...]) * (DIFF_HD ** -0.5)
    lane = lax.broadcasted_iota(jnp.int32, q.shape, 1)
    first_map = lane < DIFF_HD
    kr = kr_scr[...]
    vt = vt_scr[...]

    def fold_keys(x, op):
        part = op(x.reshape(x.shape[0] // ATTN_FOLD, ATTN_FOLD, x.shape[1]), axis=0)
        return op(part, axis=0, keepdims=True)

    n_sub = q.shape[0] // ATTN_SUB
    qms = [jnp.where(first_map if mp == 0 else jnp.logical_not(first_map), q, 0.0)[s * ATTN_SUB:(s + 1) * ATTN_SUB]
           for s in range(n_sub) for mp in range(2)]
    sts = [_dot_nt(kr, qm.astype(BF16)) for qm in qms]
    ms_ = [fold_keys(st, jnp.max) for st in sts]
    ps = [jnp.exp(st - m) for st, m in zip(sts, ms_)]
    ls = [fold_keys(p, jnp.sum) for p in ps]
    os_ = [_dot(vt, p.astype(BF16)) / l for p, l in zip(ps, ls)]
    for s in range(n_sub):
        ot = os_[2 * s] - lam * os_[2 * s + 1]
        ms = jnp.mean(ot * ot, axis=0, keepdims=True)
        o_ref[s * ATTN_SUB:(s + 1) * ATTN_SUB, :] = (ot * lax.rsqrt(ms + LN_EPS) * g_ref[...] * (1.0 - lambda_init)).T


def _diff_attention(proj3, cos_t, sin_t, diff_lambda, subln_g, lambda_init):
    bsz, seq, _ = proj3.shape
    tq = min(ATTN_TQ, seq)
    cq, ck, cv = COL_Q // LANES, COL_K // LANES, COL_V // LANES
    return pl.pallas_call(
        functools.partial(_attn_body, lambda_init=lambda_init),
        grid=(bsz, DIFF_HEADS, seq // tq),
        in_specs=[pl.BlockSpec((None, tq, LANES), lambda b, h, i: (b, i, cq + h)),
                  pl.BlockSpec((None, seq, LANES), lambda b, h, i: (b, 0, ck + h)),
                  pl.BlockSpec((None, seq, LANES), lambda b, h, i: (b, 0, cv + h)),
                  pl.BlockSpec((tq, LANES), lambda b, h, i: (i, 0)),
                  pl.BlockSpec((tq, LANES), lambda b, h, i: (i, 0)),
                  pl.BlockSpec((seq, LANES), lambda b, h, i: (0, 0)),
                  pl.BlockSpec((seq, LANES), lambda b, h, i: (0, 0)),
                  pl.BlockSpec((4, DIFF_HD), lambda b, h, i: (0, 0)),
                  pl.BlockSpec((DIFF_VD, 1), lambda b, h, i: (0, 0))],
        out_specs=pl.BlockSpec((None, tq, LANES), lambda b, h, i: (b, i, h)),
        out_shape=jax.ShapeDtypeStruct((bsz, seq, DIFF_HEADS * DIFF_VD), F32),
        scratch_shapes=[pltpu.VMEM((seq, LANES), BF16), pltpu.VMEM((LANES, seq), BF16)],
        name="diff_attention",
        compiler_params=_cparams(("parallel", "parallel", "arbitrary")),
    )(proj3, proj3, proj3, cos_t, sin_t, cos_t, sin_t, diff_lambda, subln_g)


DN_PAD = 8
DN_ROWS = 256
DN_BLK = 16
PAIR = 2 * DN_CHUNK
DN_UNROLL = 4


def _dn_body(q_ref, k_ref, v_ref, z_ref, gt_ref, cwq_ref, cwk_ref, cwv_ref, prm_ref, ng_ref, o_ref,
             pad_scr, q_scr, k_scr, v_scr, g_scr, dec_scr, m_scr, n_scr, qe_scr, oi_scr, *, seq):
    c = DN_CHUNK
    n_chunks = seq // c
    ones_bd = _ones_blockdiag(LANES)
    zeros = jnp.zeros((DN_PAD, LANES), F32)
    pad_scr[0:DN_PAD, :] = zeros
    pad_scr[DN_PAD + seq:2 * DN_PAD + seq, :] = zeros
    half = (DN_CONV - 1) // 2

    def conv_silu(src_ref, cw_ref, dst_scr, normalise):
        def copy(i, carry):
            base = pl.multiple_of(i * DN_ROWS, DN_ROWS)
            pad_scr[pl.ds(base + DN_PAD, DN_ROWS), :] = src_ref[pl.ds(base, DN_ROWS), :]
            return carry

        lax.fori_loop(0, seq // DN_ROWS, copy, 0)

        def conv(i, carry):
            base = pl.multiple_of(i * DN_ROWS, DN_ROWS)
            win = pad_scr[pl.ds(base, DN_ROWS + 2 * DN_PAD), :]
            acc = win[DN_PAD - half:DN_PAD - half + DN_ROWS, :] * cw_ref[0:1, :]
            for j in range(1, DN_CONV):
                off = DN_PAD - half + j
                acc = acc + win[off:off + DN_ROWS, :] * cw_ref[j:j + 1, :]
            y = _silu(acc)
            if normalise:
                y = y * lax.rsqrt(_group_sum64(y * y, ones_bd) + 1e-6)
            dst_scr[pl.ds(base, DN_ROWS), :] = y
            return carry

        lax.fori_loop(0, seq // DN_ROWS, conv, 0)

    conv_silu(q_ref, cwq_ref, q_scr, True)
    conv_silu(k_ref, cwk_ref, k_scr, True)
    conv_silu(v_ref, cwv_ref, v_scr, False)

    a_log = prm_ref[0:1, :]
    dt_bias = prm_ref[1:2, :]

    def gates(i, carry):
        base = pl.multiple_of(i * DN_ROWS, DN_ROWS)
        blk = gt_ref[pl.ds(base, DN_ROWS), :]
        lane = lax.broadcasted_iota(jnp.int32, blk.shape, 1)
        g_scr[pl.ds(base, DN_ROWS), :] = jnp.where(lane < 4, _sigmoid(blk),
                                                   -jnp.exp(a_log) * _softplus(blk + dt_bias))
        return carry

    lax.fori_loop(0, seq // DN_ROWS, gates, 0)

    row2 = lax.broadcasted_iota(jnp.int32, (PAIR, PAIR), 0)
    col2 = lax.broadcasted_iota(jnp.int32, (PAIR, PAIR), 1)
    same_head = (row2 // c) == (col2 // c)
    same_blk = (row2 // DN_BLK) == (col2 // DN_BLK)
    lane_cl = lax.broadcasted_iota(jnp.int32, (c, LANES), 1)
    row_cl = lax.broadcasted_iota(jnp.int32, (c, LANES), 0)
    head0 = lane_cl < DN_HD

    def stack(x):
        return jnp.concatenate([jnp.where(head0, x, 0.0), jnp.where(head0, 0.0, x)], axis=0)

    def fold(x):
        return x[:c] + x[c:]

    def col_pair(x, lane0):
        return jnp.concatenate([x[:, lane0:lane0 + 1], x[:, lane0 + 1:lane0 + 2]], axis=0)

    bd_state = (lax.broadcasted_iota(jnp.int32, (LANES, LANES), 0) // DN_HD) == \
               (lax.broadcasted_iota(jnp.int32, (LANES, LANES), 1) // DN_HD)

    def phase1_chunk(n):
        rows = pl.ds(pl.multiple_of(n * c, c), c)
        gc = g_scr[rows, :]
        fwd_cum, bwd_cum = gc, gc
        sh = 1
        while sh < c:
            fwd_cum = fwd_cum + jnp.where(row_cl >= sh, pltpu.roll(fwd_cum, sh, 0), 0.0)
            bwd_cum = bwd_cum + jnp.where(row_cl < c - sh, pltpu.roll(bwd_cum, c - sh, 0), 0.0)
            sh *= 2
        cum = jnp.where(lane_cl < 6, fwd_cum, bwd_cum)
        cum_t = cum.T
        tot = jnp.where(lane_cl[0:1] < 6, cum[c - 1:c, :], cum[0:1, :])

        kc = k_scr[rows, :]
        qc = q_scr[rows, :]
        vc = v_scr[rows, :]
        k2 = stack(kc)
        q2 = stack(qc)
        v2 = stack(vc)
        k2b = k2.astype(BF16)
        kq = _dot_nt(jnp.concatenate([k2b, q2.astype(BF16)], axis=0), k2b)
        kk = kq[:PAIR]
        qk = kq[PAIR:]
        return dict(n=n, rows=rows, gc=gc, cum=cum, cum_t=cum_t, tot=tot, k2=k2, q2=q2, v2=v2, kk=kk, qk=qk)

    def phase1_chain(ch, d):
        gc, cum, cum_t, tot = ch["gc"], ch["cum"], ch["cum_t"], ch["tot"]
        beta2 = col_pair(gc, 2 * d)
        cum2 = col_pair(cum, 4 + 2 * d)
        cum_row = jnp.concatenate([cum_t[4 + 2 * d:5 + 2 * d, :], cum_t[5 + 2 * d:6 + 2 * d, :]], axis=1)
        tot2 = jnp.concatenate([jnp.broadcast_to(tot[:, 4 + 2 * d:5 + 2 * d], (c, 1)),
                                jnp.broadcast_to(tot[:, 5 + 2 * d:6 + 2 * d], (c, 1))], axis=0)
        if d == 0:
            incl, strict = same_head & (row2 >= col2), same_head & (row2 > col2)
        else:
            incl, strict = same_head & (row2 <= col2), same_head & (row2 < col2)
        decay = jnp.exp(jnp.where(incl, cum2 - cum_row, -jnp.inf))
        lmat = jnp.where(strict, beta2 * ch["kk"] * decay, 0.0)
        rhs = jnp.concatenate([ch["v2"] * beta2, ch["k2"] * (beta2 * jnp.exp(cum2))], axis=1)
        qk_d = jnp.where(incl, ch["qk"] * decay, 0.0) * (DN_HD ** -0.5)
        qd2 = ch["q2"] * ((DN_HD ** -0.5) * jnp.exp(cum2))
        kd = fold(ch["k2"] * jnp.exp(tot2 - cum2))
        dec = jnp.where(lane_cl[0:1] < DN_HD, jnp.exp(tot[:, 4 + 2 * d:5 + 2 * d]),
                        jnp.exp(tot[:, 5 + 2 * d:6 + 2 * d]))
        return dict(d=d, n=ch["n"], rows=ch["rows"], lmat=lmat, rhs=rhs, qk_d=qk_d, qd2=qd2, kd=kd, dec=dec)

    def phase1(i, carry):
        chunks = [phase1_chunk(i * DN_UNROLL + j) for j in range(DN_UNROLL)]
        chains = [phase1_chain(ch, d) for ch in chunks for d in range(2)]
        xs = [jnp.where(same_blk, -t["lmat"], 0.0) for t in chains]
        zs = [jnp.concatenate([jnp.where(same_blk, 0.0, t["lmat"]), t["rhs"]], axis=1) for t in chains]
        bdot = lambda a, b: _dot(a.astype(BF16), b.astype(BF16))
        zs = [z + bdot(x, z) for x, z in zip(xs, zs)]
        for _ in range(int(math.log2(DN_BLK)) - 1):
            xs = [bdot(x, x) for x in xs]
            zs = [z + bdot(x, z) for x, z in zip(xs, zs)]
        ys = [-z[:, :PAIR] for z in zs]
        rs = [z[:, PAIR:] for z in zs]
        rs = [r + bdot(y, r) for y, r in zip(ys, rs)]
        for _ in range(int(math.log2(DN_CHUNK // DN_BLK)) - 1):
            ys = [bdot(y, y) for y in ys]
            rs = [r + bdot(y, r) for y, r in zip(ys, rs)]
        a_s = [_dot(t["qk_d"].astype(BF16), sol.astype(BF16)) for t, sol in zip(chains, rs)]
        mns = [_dot_tn(t["kd"].astype(BF16), fold(sol).astype(BF16)) for t, sol in zip(chains, rs)]
        for t, a, mn in zip(chains, a_s, mns):
            d, n, rows = t["d"], t["n"], t["rows"]
            n_scr[d, n] = jnp.where(bd_state, mn[:, :LANES], 0.0)
            m_scr[d, n] = jnp.where(bd_state, -mn[:, LANES:], 0.0).astype(BF16)
            qe_scr[d, rows, :] = fold(t["qd2"] - a[:, LANES:]).astype(BF16)
            oi_scr[d, rows, :] = fold(a[:, :LANES])
            dec_scr[d, n] = jnp.broadcast_to(t["dec"], (SUBLANES, LANES))
        return carry

    lax.fori_loop(0, n_chunks // DN_UNROLL, phase1, 0)

    o_ref[...] = jnp.zeros((seq, LANES), F32)

    def phase2(i, states):
        new_states = []
        for d in range(2):
            n = i if d == 0 else n_chunks - 1 - i
            rows = pl.ds(pl.multiple_of(n * c, c), c)
            state = states[d]
            sb = state.astype(BF16)
            o_ref[rows, :] = o_ref[rows, :] + oi_scr[d, rows, :] + _dot(qe_scr[d, rows, :], sb)
            new_states.append(state * dec_scr[d, n][0:1, :] + _dot(m_scr[d, n], sb) + n_scr[d, n])
        return tuple(new_states)

    zero_state = jnp.zeros((LANES, LANES), F32)
    lax.fori_loop(0, n_chunks, phase2, (zero_state, zero_state))

    def finish(i, carry):
        base = pl.multiple_of(i * DN_ROWS, DN_ROWS)
        o = o_ref[pl.ds(base, DN_ROWS), :]
        ms = _group_sum64(o * o, ones_bd) * (1.0 / DN_HD)
        o_ref[pl.ds(base, DN_ROWS), :] = o * lax.rsqrt(ms + LN_EPS) * ng_ref[...] * _silu(z_ref[pl.ds(base, DN_ROWS), :])
        return carry

    lax.fori_loop(0, seq // DN_ROWS, finish, 0)


def _deltanet(proj3, dn_conv, prm, norm_g2):
    bsz, seq, _ = proj3.shape
    cq, ck, cv, cz, cg = (COL_DNQ // LANES, COL_DNK // LANES, COL_DNV // LANES, COL_DNZ // LANES, COL_GATE // LANES)
    col = lambda c0: pl.BlockSpec((None, seq, LANES), lambda b, hp: (b, 0, c0 + hp))
    cw = lambda c0: pl.BlockSpec((DN_CONV, LANES), lambda b, hp: (0, c0 + hp))
    n_chunks = seq // DN_CHUNK
    return pl.pallas_call(
        functools.partial(_dn_body, seq=seq),
        grid=(bsz, DN_HEADS // 2),
        in_specs=[col(cq), col(ck), col(cv), col(cz), col(cg), cw(0), cw(2), cw(4),
                  pl.BlockSpec((None, SUBLANES, LANES), lambda b, hp: (hp, 0, 0)),
                  pl.BlockSpec((1, LANES), lambda b, hp: (0, 0))],
        out_specs=pl.BlockSpec((None, seq, LANES), lambda b, hp: (b, 0, hp)),
        out_shape=jax.ShapeDtypeStruct((bsz, seq, DN_HEADS * DN_HD), F32),
        scratch_shapes=[pltpu.VMEM((seq + 2 * DN_PAD, LANES), F32),
                        pltpu.VMEM((seq, LANES), F32), pltpu.VMEM((seq, LANES), F32), pltpu.VMEM((seq, LANES), F32),
                        pltpu.VMEM((seq, LANES), F32),
                        pltpu.VMEM((2, n_chunks, SUBLANES, LANES), F32),
                        pltpu.VMEM((2, n_chunks, LANES, LANES), BF16),
                        pltpu.VMEM((2, n_chunks, LANES, LANES), F32),
                        pltpu.VMEM((2, seq, LANES), BF16),
                        pltpu.VMEM((2, seq, LANES), F32)],
        name="deltanet",
        compiler_params=_cparams(("parallel", "parallel")),
    )(proj3, proj3, proj3, proj3, proj3, dn_conv, dn_conv, dn_conv, prm, norm_g2)


def _outproj_body(yc_ref, yd_ref, yn_ref, x_ref, w_ref, g_ref, b_ref, o_ref):
    mix = jnp.concatenate([yc_ref[...], yd_ref[...], yn_ref[...]], axis=-1).astype(BF16)
    h = _dot(mix, w_ref[...])
    o_ref[...] = _layer_norm(DEEPNORM_ALPHA * x_ref[...] + h, g_ref[...], b_ref[...])


def _outproj_ln(yc, yd, yn, x2, w, g, b):
    t, d = x2.shape
    tm = min(512, t)
    row = lambda n: pl.BlockSpec((tm, n), lambda i: (i, 0))
    full = lambda shape: pl.BlockSpec(shape, lambda i: (0,) * len(shape))
    return pl.pallas_call(
        _outproj_body,
        grid=(t // tm,),
        in_specs=[row(yc.shape[1]), row(yd.shape[1]), row(yn.shape[1]), row(d), full(w.shape), full((1, d)), full((1, d))],
        out_specs=row(d),
        out_shape=jax.ShapeDtypeStruct((t, d), F32),
        name="outproj_ln",
        compiler_params=_cparams(("parallel",)),
    )(yc, yd, yn, x2, w, g, b)


def _ffn_body(x_ref, w1_ref, w3_ref, w2_ref, g_ref, b_ref, o_ref):
    x = x_ref[...]
    xb = x.astype(BF16)
    h = _silu(_dot(xb, w1_ref[...])) * _dot(xb, w3_ref[...])
    f = _dot(h.astype(BF16), w2_ref[...])
    o_ref[...] = _layer_norm(DEEPNORM_ALPHA * x + f, g_ref[...], b_ref[...])


def _ffn_ln(x2, w1, w3, w2, g, b):
    t, d = x2.shape
    f = w1.shape[1]
    tm = min(512, t)
    row = pl.BlockSpec((tm, d), lambda i: (i, 0))
    once = lambda shape: pl.BlockSpec(shape, lambda i: (0,) * len(shape), pipeline_mode=pl.Buffered(1))
    return pl.pallas_call(
        _ffn_body,
        grid=(t // tm,),
        in_specs=[row, once((d, f)), once((d, f)), once((f, d)), once((1, d)), once((1, d))],
        out_specs=row,
        out_shape=jax.ShapeDtypeStruct((t, d), F32),
        name="ffn_ln",
        compiler_params=_cparams(("parallel",)),
    )(x2, w1, w3, w2, g, b)


MOE_TB = 512
MOE_RUN = SUBLANES
MOE_BUF = 2 * MOE_TB + N_EXPERTS * MOE_RUN
META_P1, META_P2, META_G1, META_G2 = range(4)


def _moe_slots(t):
    n_blocks = t // min(MOE_TB, t)
    raw = 2 * t + n_blocks * N_EXPERTS * (MOE_RUN - 1) + N_EXPERTS * (MOE_TM - 1)
    return ((raw + MOE_TM - 1) // MOE_TM) * MOE_TM


def _router_body(x_ref, rw_ref, meta_ref, cnt_ref):
    tm = x_ref.shape[0]
    logits = jnp.dot(x_ref[...], rw_ref[...], preferred_element_type=F32, precision=lax.Precision.HIGHEST)
    lane = lax.broadcasted_iota(jnp.int32, logits.shape, 1)
    logits = jnp.where(lane < N_EXPERTS, logits, -jnp.inf)
    m1 = jnp.max(logits, axis=-1, keepdims=True)
    e1 = jnp.min(jnp.where(logits == m1, lane, LANES), axis=-1, keepdims=True)
    rest = jnp.where(lane == e1, -jnp.inf, logits)
    m2 = jnp.max(rest, axis=-1, keepdims=True)
    e2 = jnp.min(jnp.where(rest == m2, lane, LANES), axis=-1, keepdims=True)
    t = jnp.exp(m2 - m1)
    g1 = 1.0 / (1.0 + t)
    g2 = t / (1.0 + t)

    sel = jnp.where((lane == e1) | (lane == e2), 1.0, 0.0)
    r = lax.broadcasted_iota(jnp.int32, (tm, tm), 0)
    c = lax.broadcasted_iota(jnp.int32, (tm, tm), 1)
    strict_lower = jnp.where(r > c, 1.0, 0.0).astype(BF16)
    rank = _dot(strict_lower, sel.astype(BF16))
    cnt = jnp.sum(sel, axis=0, keepdims=True)
    run = jnp.floor((cnt + (MOE_RUN - 1.0)) * (1.0 / MOE_RUN)) * MOE_RUN
    lane1 = lane[0:1]
    start = run
    sh = 1
    while sh < N_EXPERTS:
        start = start + jnp.where(lane1 >= sh, pltpu.roll(start, sh, 1), 0.0)
        sh *= 2
    pos = rank + (start - run)
    p1 = jnp.sum(jnp.where(lane == e1, pos, 0.0), axis=-1, keepdims=True)
    p2 = jnp.sum(jnp.where(lane == e2, pos, 0.0), axis=-1, keepdims=True)
    cnt_ref[...] = jnp.broadcast_to(cnt, cnt_ref.shape)

    meta = jnp.where(lane == META_P1, p1, 0.0)
    meta = jnp.where(lane == META_P2, p2, meta)
    meta = jnp.where(lane == META_G1, g1, meta)
    meta = jnp.where(lane == META_G2, g2, meta)
    meta_ref[...] = meta


def _router(x2, rw_pad):
    t, d = x2.shape
    tm = min(MOE_TB, t)
    return pl.pallas_call(
        _router_body,
        grid=(t // tm,),
        in_specs=[pl.BlockSpec((tm, d), lambda i: (i, 0)), pl.BlockSpec((d, LANES), lambda i: (0, 0))],
        out_specs=[pl.BlockSpec((tm, LANES), lambda i: (i, 0)),
                   pl.BlockSpec((None, SUBLANES, LANES), lambda i: (i, 0, 0))],
        out_shape=[jax.ShapeDtypeStruct((t, LANES), F32),
                   jax.ShapeDtypeStruct((t // tm, SUBLANES, LANES), F32)],
        name="router",
        compiler_params=_cparams(("parallel",)),
    )(x2, rw_pad)


def _for_each_run_piece(tab_ref, n_runs, blk, fn):
    for e in range(N_EXPERTS):
        k = blk * N_EXPERTS + e
        lo, run, dst = tab_ref[k], tab_ref[n_runs + k], tab_ref[2 * n_runs + k]
        bit = MOE_TB
        while bit >= MOE_RUN:
            done = run & ~(2 * bit - 1)

            @pl.when((run & bit) != 0)
            def _():
                fn(pl.multiple_of(lo + done, MOE_RUN), pl.multiple_of(dst + done, MOE_RUN), bit)

            bit //= 2


def _dispatch_body(tab_ref, x_ref, meta_ref, xs_in_ref, xs_ref, buf, sem, *, n_runs):
    del xs_in_ref
    blk = pl.program_id(0)
    tb = x_ref.shape[0]
    meta_t = meta_ref[...].T
    p1 = meta_t[META_P1:META_P1 + 1, :]
    p2 = meta_t[META_P2:META_P2 + 1, :]
    slot = lax.broadcasted_iota(jnp.int32, (MOE_BUF, tb), 0).astype(F32)
    onehot = jnp.where((slot == p1) | (slot == p2), 1.0, 0.0).astype(BF16)
    buf[...] = _dot(onehot, x_ref[...].astype(BF16))

    def copy(buf_row, sorted_row, n):
        return pltpu.make_async_copy(buf.at[pl.ds(buf_row, n), :], xs_ref.at[pl.ds(sorted_row, n), :], sem)

    _for_each_run_piece(tab_ref, n_runs, blk, lambda *a: copy(*a).start())
    _for_each_run_piece(tab_ref, n_runs, blk, lambda *a: copy(*a).wait())


def _dispatch(tab, x2, meta, n_slots):
    t, d = x2.shape
    tb = min(MOE_TB, t)
    zeros = jnp.zeros((n_slots, d), F32)
    grid_spec = pltpu.PrefetchScalarGridSpec(
        num_scalar_prefetch=1,
        grid=(t // tb,),
        in_specs=[pl.BlockSpec((tb, d), lambda i, tab: (i, 0)),
                  pl.BlockSpec((tb, LANES), lambda i, tab: (i, 0)),
                  pl.BlockSpec(memory_space=pl.ANY)],
        out_specs=pl.BlockSpec(memory_space=pl.ANY),
        scratch_shapes=[pltpu.VMEM((MOE_BUF, d), F32), pltpu.SemaphoreType.DMA],
    )
    return pl.pallas_call(
        functools.partial(_dispatch_body, n_runs=(t // tb) * N_EXPERTS),
        grid_spec=grid_spec,
        out_shape=jax.ShapeDtypeStruct((n_slots, d), F32),
        input_output_aliases={3: 0},
        name="moe_dispatch",
        compiler_params=pltpu.CompilerParams(dimension_semantics=("arbitrary",), vmem_limit_bytes=VMEM_LIMIT,
                                             has_side_effects=True),
    )(tab, x2, meta, zeros)


def _experts_body(te_ref, nu_ref, xs_ref, w1_ref, w3_ref, w2_ref, ys_ref, xb_scr, acc_scr):
    i = pl.program_id(0)
    f = pl.program_id(1)
    used = i < nu_ref[0]

    @pl.when(used & (f == 0))
    def _():
        xb_scr[...] = xs_ref[...].astype(BF16)

    @pl.when(used)
    def _():
        xb = xb_scr[...]
        h = _silu(_dot(xb, w1_ref[...])) * _dot(xb, w3_ref[...])
        part = _dot(h.astype(BF16), w2_ref[...])

        @pl.when(f == 0)
        def _():
            acc_scr[...] = part

        @pl.when(f > 0)
        def _():
            acc_scr[...] = acc_scr[...] + part

    @pl.when(f == pl.num_programs(1) - 1)
    def _():
        @pl.when(used)
        def _():
            ys_ref[...] = acc_scr[...]

        @pl.when(jnp.logical_not(used))
        def _():
            ys_ref[...] = jnp.zeros_like(ys_ref)


def _experts(tile_expert, n_used, xs, w1, w3, w2):
    n_slots, d = xs.shape
    n_tiles = n_slots // MOE_TM
    nf = w1.shape[2] // MOE_TF

    def fidx(i, f, te, nu):
        return jnp.where(i < nu[0], f, nf - 1)

    grid_spec = pltpu.PrefetchScalarGridSpec(
        num_scalar_prefetch=2,
        grid=(n_tiles, nf),
        in_specs=[pl.BlockSpec((MOE_TM, d), lambda i, f, te, nu: (i, 0)),
                  pl.BlockSpec((None, d, MOE_TF), lambda i, f, te, nu: (te[i], 0, fidx(i, f, te, nu))),
                  pl.BlockSpec((None, d, MOE_TF), lambda i, f, te, nu: (te[i], 0, fidx(i, f, te, nu))),
                  pl.BlockSpec((None, MOE_TF, d), lambda i, f, te, nu: (te[i], fidx(i, f, te, nu), 0))],
        out_specs=pl.BlockSpec((MOE_TM, d), lambda i, f, te, nu: (i, 0)),
        scratch_shapes=[pltpu.VMEM((MOE_TM, d), BF16), pltpu.VMEM((MOE_TM, d), F32)],
    )
    return pl.pallas_call(
        _experts_body,
        grid_spec=grid_spec,
        out_shape=jax.ShapeDtypeStruct((n_slots, d), F32),
        name="moe_experts",
        compiler_params=_cparams(("arbitrary", "arbitrary")),
    )(tile_expert, n_used, xs, w1, w3, w2)


def _split_bf16(v):
    hi = v.astype(BF16)
    return hi, (v - hi.astype(F32)).astype(BF16)


def _combine_body(tab_ref, x_ref, meta_ref, ys_ref, g_ref, b_ref, o_ref, ybuf, sem, *, n_runs):
    blk = pl.program_id(0)
    tb = x_ref.shape[0]

    @pl.when(blk == 0)
    def _():
        ybuf[...] = jnp.zeros_like(ybuf)

    def copy(buf_row, sorted_row, n):
        return pltpu.make_async_copy(ys_ref.at[pl.ds(sorted_row, n), :], ybuf.at[pl.ds(buf_row, n), :], sem)

    _for_each_run_piece(tab_ref, n_runs, blk, lambda *a: copy(*a).start())
    _for_each_run_piece(tab_ref, n_runs, blk, lambda *a: copy(*a).wait())

    meta = meta_ref[...]
    p1 = meta[:, META_P1:META_P1 + 1]
    p2 = meta[:, META_P2:META_P2 + 1]
    g1 = meta[:, META_G1:META_G1 + 1]
    g2 = meta[:, META_G2:META_G2 + 1]
    slot = lax.broadcasted_iota(jnp.int32, (tb, MOE_BUF), 1).astype(F32)
    weights = jnp.where(slot == p1, g1, 0.0) + jnp.where(slot == p2, g2, 0.0)
    wh, wl = _split_bf16(weights)
    yh, yl = _split_bf16(ybuf[...])
    f = _dot(wh, yh) + _dot(wh, yl) + _dot(wl, yh)
    o_ref[...] = _layer_norm(DEEPNORM_ALPHA * x_ref[...] + f, g_ref[...], b_ref[...])


def _combine_ln(tab, x2, meta, ys, g, b):
    t, d = x2.shape
    tb = min(MOE_TB, t)
    grid_spec = pltpu.PrefetchScalarGridSpec(
        num_scalar_prefetch=1,
        grid=(t // tb,),
        in_specs=[pl.BlockSpec((tb, d), lambda i, tab: (i, 0)),
                  pl.BlockSpec((tb, LANES), lambda i, tab: (i, 0)),
                  pl.BlockSpec(memory_space=pl.ANY),
                  pl.BlockSpec((1, d), lambda i, tab: (0, 0)),
                  pl.BlockSpec((1, d), lambda i, tab: (0, 0))],
        out_specs=pl.BlockSpec((tb, d), lambda i, tab: (i, 0)),
        scratch_shapes=[pltpu.VMEM((MOE_BUF, d), F32), pltpu.SemaphoreType.DMA],
    )
    return pl.pallas_call(
        functools.partial(_combine_body, n_runs=(t // tb) * N_EXPERTS),
        grid_spec=grid_spec,
        out_shape=jax.ShapeDtypeStruct((t, d), F32),
        name="moe_combine_ln",
        compiler_params=_cparams(("arbitrary",)),
    )(tab, x2, meta, ys, g, b)


def _moe_ln(x2, rw_pad, w1, w3, w2, g, b):
    t, d = x2.shape
    meta, blk_cnt = _router(x2, rw_pad)
    cnt = blk_cnt[:, 0, :N_EXPERTS].astype(jnp.int32)
    run = ((cnt + MOE_RUN - 1) // MOE_RUN) * MOE_RUN
    lo = jnp.cumsum(run, axis=1) - run
    before = jnp.cumsum(run, axis=0) - run
    padded = ((jnp.sum(run, axis=0) + MOE_TM - 1) // MOE_TM) * MOE_TM
    ends = jnp.cumsum(padded)
    starts = ends - padded
    tab = jnp.concatenate([lo.reshape(-1), run.reshape(-1), (starts[None, :] + before).reshape(-1)]).astype(jnp.int32)
    n_slots = _moe_slots(t)
    n_tiles = n_slots // MOE_TM
    n_used = (ends[-1] // MOE_TM).astype(jnp.int32)
    tile_start = jnp.arange(n_tiles, dtype=jnp.int32) * MOE_TM
    tile_expert = jnp.sum((tile_start[:, None] >= ends[None, :]).astype(jnp.int32), axis=1)
    last_expert = jnp.sum((((n_used - 1) * MOE_TM) >= ends).astype(jnp.int32))
    tile_expert = jnp.where(jnp.arange(n_tiles) < n_used, tile_expert, last_expert).astype(jnp.int32)

    xs = _dispatch(tab, x2, meta, n_slots)
    ys = _experts(tile_expert, n_used.reshape(1), xs, w1, w3, w2)
    return _combine_ln(tab, x2, meta, ys, g, b)


def _rope_tables(seq):
    inv = ROPE_THETA ** (-jnp.arange(0, DIFF_HD, 2, dtype=F32) / DIFF_HD)
    ang = jnp.arange(seq, dtype=F32)[:, None] * inv[None, :]
    cos, sin = jnp.cos(ang), jnp.sin(ang)
    cos_t = jnp.tile(cos, (1, LANES // cos.shape[1]))
    sin_t = jnp.tile(jnp.concatenate([-sin, sin], axis=1), (1, LANES // (2 * sin.shape[1])))
    return cos_t, sin_t


def _gate_columns():
    cols = []
    for hp in range(DN_HEADS // 2):
        blk = [PROJ_RAW - 16 + d * DN_HEADS + 2 * hp + hl for d in range(2) for hl in range(2)]
        blk += [PROJ_RAW - 8 + d * DN_HEADS + 2 * hp + hl for d in range(2) for hl in range(2)]
        cols.append(blk)
    return cols


def _prep_w_in(w_in_l):
    parts = [w_in_l[:, :COL_GATE]]
    for blk in _gate_columns():
        parts.append(w_in_l[:, jnp.array(blk)])
        parts.append(jnp.zeros((w_in_l.shape[0], LANES - len(blk)), w_in_l.dtype))
    return jnp.concatenate(parts, axis=1).astype(BF16)


def _prep_dn_params(a_log_l, dt_bias_l):
    out = []
    for hp in range(DN_HEADS // 2):
        idx = [(d, 2 * hp + hl) for d in range(2) for hl in range(2)]
        a = jnp.stack([a_log_l[d, h] for d, h in idx])
        t = jnp.stack([dt_bias_l[d, h] for d, h in idx])
        blk = jnp.zeros((SUBLANES, LANES), F32)
        blk = blk.at[0, 4:8].set(a).at[1, 4:8].set(t)
        out.append(blk)
    return jnp.stack(out)


def kernel(x, w_in, w_o, ln1_g, ln1_b, ln2_g, ln2_b, conv_dw, conv_dw_b, conv_ln_g, conv_ln_b, conv_pw,
           diff_lambda, diff_subln_g, dn_conv, dn_a_log, dn_dt_bias, dn_norm_g,
           ffn_w1, ffn_w3, ffn_w2, router_w, moe_w1, moe_w3, moe_w2):
    bsz, seq, d = x.shape
    t = bsz * seq
    cos_t, sin_t = _rope_tables(seq)
    x2 = x.reshape(t, d)
    row = lambda v: v.reshape(1, -1)
    for layer in range(DEPTH):
        lambda_init = 0.8 - 0.6 * math.exp(-0.3 * layer)
        proj = _inproj(x2, _prep_w_in(w_in[layer]))
        proj3 = proj.reshape(bsz, seq, PROJ_PAD)
        y_conv = _conv_module(proj3, conv_dw[layer], row(conv_dw_b[layer]), row(conv_ln_g[layer]),
                              row(conv_ln_b[layer]), conv_pw[layer].astype(BF16))
        y_diff = _diff_attention(proj3, cos_t, sin_t, diff_lambda[layer], diff_subln_g[layer].reshape(-1, 1),
                                 lambda_init)
        y_dn = _deltanet(proj3, dn_conv[layer], _prep_dn_params(dn_a_log[layer], dn_dt_bias[layer]),
                         row(jnp.tile(dn_norm_g[layer], 2)))
        x2 = _outproj_ln(y_conv.reshape(t, -1), y_diff.reshape(t, -1), y_dn.reshape(t, -1), x2,
                         w_o[layer].astype(BF16), row(ln1_g[layer]), row(ln1_b[layer]))
        j = layer // 2
        if layer % 2 == 0:
            x2 = _ffn_ln(x2, ffn_w1[j].astype(BF16), ffn_w3[j].astype(BF16), ffn_w2[j].astype(BF16),
                         row(ln2_g[layer]), row(ln2_b[layer]))
        else:
            rw_pad = jnp.pad(router_w[j], ((0, 0), (0, LANES - N_EXPERTS)))
            x2 = _moe_ln(x2, rw_pad, moe_w1[j].astype(BF16), moe_w3[j].astype(BF16), moe_w2[j].astype(BF16),
                         row(ln2_g[layer]), row(ln2_b[layer]))
    return x2.reshape(bsz, seq, d)
```

```python
import functools
import math

import jax
import jax.numpy as jnp
from jax import lax
from jax.experimental import pallas as pl
from jax.experimental.pallas import tpu as pltpu

F32 = jnp.float32
BF16 = jnp.bfloat16

D_MODEL = 1024
DEPTH = 4
CONV_W = D_MODEL // 4
CONV_WIDTH = 31
DIFF_HEADS = 4
DIFF_HD = D_MODEL // 16
DIFF_VD = 2 * DIFF_HD
DN_HEADS = 4
DN_HD = D_MODEL // 16
DN_CONV = 5
DN_CHUNK = 64
ROPE_THETA = 10000.0
D_FF = 11 * D_MODEL // 4
N_EXPERTS = 8
D_FF_EXPERT = 7 * D_MODEL // 2
DEEPNORM_ALPHA = (2 * DEPTH) ** 0.25
LN_EPS = 1e-5

LANES = 128
SUBLANES = 8
VMEM_LIMIT = 56 * 2 ** 20

COL_CONV = 0
COL_Q = 512
COL_K = 1024
COL_V = 1536
COL_DNQ = 2048
COL_DNK = 2304
COL_DNV = 2560
COL_DNZ = 2816
COL_GATE = 3072
PROJ_RAW = 3088
PROJ_PAD = COL_GATE + 2 * LANES

MOE_TM = 512
MOE_TF = 1792


def _cparams(sem):
    return pltpu.CompilerParams(dimension_semantics=sem, vmem_limit_bytes=VMEM_LIMIT)


def _sigmoid(x):
    return 1.0 / (1.0 + jnp.exp(-x))


def _silu(x):
    return x * _sigmoid(x)


def _softplus(x):
    return jnp.maximum(x, 0.0) + jnp.log1p(jnp.exp(-jnp.abs(x)))


def _layer_norm(x, g, b):
    mu = jnp.mean(x, axis=-1, keepdims=True)
    xc = x - mu
    var = jnp.mean(xc * xc, axis=-1, keepdims=True)
    return xc * lax.rsqrt(var + LN_EPS) * g + b


def _dot(a, b):
    return jnp.dot(a, b, preferred_element_type=F32)


def _dot_nt(a, b):
    return lax.dot_general(a, b, (((1,), (1,)), ((), ())), preferred_element_type=F32)


def _dot_tn(a, b):
    return lax.dot_general(a, b, (((0,), (0,)), ((), ())), preferred_element_type=F32)


def _group_sum64(x, ones_bd):
    hi = x.astype(BF16)
    lo = (x - hi.astype(F32)).astype(BF16)
    return _dot(hi, ones_bd) + _dot(lo, ones_bd)


def _ones_blockdiag(n):
    r = lax.broadcasted_iota(jnp.int32, (n, n), 0) // DN_HD
    c = lax.broadcasted_iota(jnp.int32, (n, n), 1) // DN_HD
    return jnp.where(r == c, 1.0, 0.0).astype(BF16)


def _inproj_body(x_ref, w_ref, o_ref):
    o_ref[...] = _dot(x_ref[...].astype(BF16), w_ref[...])


def _inproj(x2, w):
    t, d = x2.shape
    n = w.shape[1]
    tm = min(512, t)
    return pl.pallas_call(
        _inproj_body,
        grid=(t // tm,),
        in_specs=[pl.BlockSpec((tm, d), lambda i: (i, 0)),
                  pl.BlockSpec((d, n), lambda i: (0, 0))],
        out_specs=pl.BlockSpec((tm, n), lambda i: (i, 0)),
        out_shape=jax.ShapeDtypeStruct((t, n), F32),
        name="inproj",
        compiler_params=_cparams(("parallel",)),
    )(x2, w)


CONV_PAD = 16
CONV_ROWS = 128


def _conv_body(p_ref, dw_ref, dwb_ref, g_ref, b_ref, pw_ref, o_ref, pad_scr, *, seq):
    zeros = jnp.zeros((CONV_PAD, CONV_W), F32)
    pad_scr[0:CONV_PAD, :] = zeros
    pad_scr[CONV_PAD + seq:2 * CONV_PAD + seq, :] = zeros

    def glu(i, carry):
        base = pl.multiple_of(i * CONV_ROWS, CONV_ROWS)
        p = p_ref[pl.ds(base, CONV_ROWS), :]
        pad_scr[pl.ds(base + CONV_PAD, CONV_ROWS), :] = p[:, :CONV_W] * _sigmoid(p[:, CONV_W:])
        return carry

    lax.fori_loop(0, seq // CONV_ROWS, glu, 0)
    half = (CONV_WIDTH - 1) // 2

    def conv(i, carry):
        base = pl.multiple_of(i * CONV_ROWS, CONV_ROWS)
        acc = jnp.zeros((CONV_ROWS, CONV_W), F32) + dwb_ref[...]
        win = pad_scr[pl.ds(base, CONV_ROWS + 2 * CONV_PAD), :]
        n_win = CONV_ROWS + 2 * CONV_PAD
        for r in range(SUBLANES):
            wr = win if r == 0 else pltpu.roll(win, n_win - r, 0)
            for j in range(CONV_WIDTH):
                off = CONV_PAD - half + j
                if off % SUBLANES == r:
                    acc = acc + wr[off - r:off - r + CONV_ROWS, :] * dw_ref[j:j + 1, :]
        y = _silu(_layer_norm(acc, g_ref[...], b_ref[...]))
        o_ref[pl.ds(base, CONV_ROWS), :] = _dot(y.astype(BF16), pw_ref[...])
        return carry

    lax.fori_loop(0, seq // CONV_ROWS, conv, 0)


def _conv_module(proj3, dw, dwb, g, b, pw):
    bsz, seq, _ = proj3.shape
    full = lambda shape: pl.BlockSpec(shape, lambda i: (0,) * len(shape))
    return pl.pallas_call(
        functools.partial(_conv_body, seq=seq),
        grid=(bsz,),
        in_specs=[pl.BlockSpec((None, seq, 2 * CONV_W), lambda i: (i, 0, COL_CONV // (2 * CONV_W))),
                  full((CONV_WIDTH, CONV_W)), full((1, CONV_W)), full((1, CONV_W)), full((1, CONV_W)),
                  full((CONV_W, CONV_W))],
        out_specs=pl.BlockSpec((None, seq, CONV_W), lambda i: (i, 0, 0)),
        out_shape=jax.ShapeDtypeStruct((bsz, seq, CONV_W), F32),
        scratch_shapes=[pltpu.VMEM((seq + 2 * CONV_PAD, CONV_W), F32)],
        name="conv_module",
        compiler_params=_cparams(("parallel",)),
    )(proj3, dw, dwb, g, b, pw)


ATTN_TQ = 1024
ATTN_SUB = 256
ATTN_ONES = 16
ATTN_FOLD = 64


def _rope(x, cos, sin_signed):
    lane = lax.broadcasted_iota(jnp.int32, x.shape, 1)
    first = (lane % DIFF_HD) < (DIFF_HD // 2)
    rot = jnp.where(first, pltpu.roll(x, LANES - DIFF_HD // 2, 1), pltpu.roll(x, DIFF_HD // 2, 1))
    return x * cos + rot * sin_signed


def _attn_body(q_ref, k_ref, v_ref, cq_ref, sq_ref, ck_ref, sk_ref, dl_ref, g_ref, o_ref, kr_scr, vt_scr,
               *, lambda_init):
    @pl.when(pl.program_id(2) == 0)
    def _():
        kr_scr[...] = _rope(k_ref[...], ck_ref[...], sk_ref[...]).astype(BF16)
        vt_scr[0:DIFF_VD, :] = v_ref[...].T.astype(BF16)
        vt_scr[DIFF_VD:, :] = jnp.ones((ATTN_ONES, vt_scr.shape[1]), BF16)

    dl = dl_ref[...]
    lam = (jnp.exp(jnp.sum(dl[0:1] * dl[1:2], axis=-1, keepdims=True))
           - jnp.exp(jnp.sum(dl[2:3] * dl[3:4], axis=-1, keepdims=True)) + lambda_init)

    q = _rope(q_ref[...], cq_ref[...], sq_ref[...]) * (DIFF_HD ** -0.5 * math.log2(math.e))
    lane = lax.broadcasted_iota(jnp.int32, q.shape, 1)
    first_map = lane < DIFF_HD
    kr = kr_scr[...]
    vt = vt_scr[...]

    def fold_keys(x, op):
        part = op(x.reshape(x.shape[0] // ATTN_FOLD, ATTN_FOLD, x.shape[1]), axis=0)
        return op(part, axis=0, keepdims=True)

    n_sub = q.shape[0] // ATTN_SUB
    qms = [jnp.where(first_map if mp == 0 else jnp.logical_not(first_map), q, 0.0)[s * ATTN_SUB:(s + 1) * ATTN_SUB]
           for s in range(n_sub) for mp in range(2)]
    sts = [_dot_nt(kr, qm.astype(BF16)) for qm in qms]
    ms_ = [fold_keys(st, jnp.max) for st in sts]
    ps = [jnp.exp2(st - m).astype(BF16) for st, m in zip(sts, ms_)]
    pv = [_dot(vt, p) for p in ps]
    os_ = [o[0:DIFF_VD] / o[DIFF_VD:DIFF_VD + 1] for o in pv]
    for s in range(n_sub):
        ot = os_[2 * s] - lam * os_[2 * s + 1]
        ms = jnp.mean(ot * ot, axis=0, keepdims=True)
        o_ref[s * ATTN_SUB:(s + 1) * ATTN_SUB, :] = (ot * lax.rsqrt(ms + LN_EPS) * g_ref[...] * (1.0 - lambda_init)).T


def _diff_attention(proj3, cos_t, sin_t, diff_lambda, subln_g, lambda_init):
    bsz, seq, _ = proj3.shape
    tq = min(ATTN_TQ, seq)
    cq, ck, cv = COL_Q // LANES, COL_K // LANES, COL_V // LANES
    return pl.pallas_call(
        functools.partial(_attn_body, lambda_init=lambda_init),
        grid=(bsz, DIFF_HEADS, seq // tq),
        in_specs=[pl.BlockSpec((None, tq, LANES), lambda b, h, i: (b, i, cq + h)),
                  pl.BlockSpec((None, seq, LANES), lambda b, h, i: (b, 0, ck + h)),
                  pl.BlockSpec((None, seq, LANES), lambda b, h, i: (b, 0, cv + h)),
                  pl.BlockSpec((tq, LANES), lambda b, h, i: (i, 0)),
                  pl.BlockSpec((tq, LANES), lambda b, h, i: (i, 0)),
                  pl.BlockSpec((seq, LANES), lambda b, h, i: (0, 0)),
                  pl.BlockSpec((seq, LANES), lambda b, h, i: (0, 0)),
                  pl.BlockSpec((4, DIFF_HD), lambda b, h, i: (0, 0)),
                  pl.BlockSpec((DIFF_VD, 1), lambda b, h, i: (0, 0))],
        out_specs=pl.BlockSpec((None, tq, LANES), lambda b, h, i: (b, i, h)),
        out_shape=jax.ShapeDtypeStruct((bsz, seq, DIFF_HEADS * DIFF_VD), F32),
        scratch_shapes=[pltpu.VMEM((seq, LANES), BF16), pltpu.VMEM((DIFF_VD + ATTN_ONES, seq), BF16)],
        name="diff_attention",
        compiler_params=_cparams(("parallel", "parallel", "arbitrary")),
    )(proj3, proj3, proj3, cos_t, sin_t, cos_t, sin_t, diff_lambda, subln_g)


DN_PAD = 8
DN_ROWS = 256
DN_BLK = 16
PAIR = 2 * DN_CHUNK
DN_UNROLL = 4


def _dn_body(q_ref, k_ref, v_ref, z_ref, gt_ref, cwq_ref, cwk_ref, cwv_ref, prm_ref, ng_ref, o_ref,
             pad_scr, q_scr, k_scr, v_scr, g_scr, dec_scr, m_scr, n_scr, qe_scr, oi_scr, *, seq):
    c = DN_CHUNK
    n_chunks = seq // c
    ones_bd = _ones_blockdiag(LANES)
    zeros = jnp.zeros((DN_PAD, LANES), F32)
    pad_scr[0:DN_PAD, :] = zeros
    pad_scr[DN_PAD + seq:2 * DN_PAD + seq, :] = zeros
    half = (DN_CONV - 1) // 2

    def conv_silu(src_ref, cw_ref, dst_scr, normalise):
        def copy(i, carry):
            base = pl.multiple_of(i * DN_ROWS, DN_ROWS)
            pad_scr[pl.ds(base + DN_PAD, DN_ROWS), :] = src_ref[pl.ds(base, DN_ROWS), :]
            return carry

        lax.fori_loop(0, seq // DN_ROWS, copy, 0)

        def conv(i, carry):
            base = pl.multiple_of(i * DN_ROWS, DN_ROWS)
            win = pad_scr[pl.ds(base, DN_ROWS + 2 * DN_PAD), :]
            acc = win[DN_PAD - half:DN_PAD - half + DN_ROWS, :] * cw_ref[0:1, :]
            for j in range(1, DN_CONV):
                off = DN_PAD - half + j
                acc = acc + win[off:off + DN_ROWS, :] * cw_ref[j:j + 1, :]
            y = _silu(acc)
            if normalise:
                y = y * lax.rsqrt(_group_sum64(y * y, ones_bd) + 1e-6)
            dst_scr[pl.ds(base, DN_ROWS), :] = y
            return carry

        lax.fori_loop(0, seq // DN_ROWS, conv, 0)

    conv_silu(q_ref, cwq_ref, q_scr, True)
    conv_silu(k_ref, cwk_ref, k_scr, True)
    conv_silu(v_ref, cwv_ref, v_scr, False)

    a_log = prm_ref[0:1, :]
    dt_bias = prm_ref[1:2, :]

    def gates(i, carry):
        base = pl.multiple_of(i * DN_ROWS, DN_ROWS)
        blk = gt_ref[pl.ds(base, DN_ROWS), :]
        lane = lax.broadcasted_iota(jnp.int32, blk.shape, 1)
        g_scr[pl.ds(base, DN_ROWS), :] = jnp.where(lane < 4, _sigmoid(blk),
                                                   -jnp.exp(a_log) * _softplus(blk + dt_bias))
        return carry

    lax.fori_loop(0, seq // DN_ROWS, gates, 0)

    row2 = lax.broadcasted_iota(jnp.int32, (PAIR, PAIR), 0)
    col2 = lax.broadcasted_iota(jnp.int32, (PAIR, PAIR), 1)
    same_head = (row2 // c) == (col2 // c)
    same_blk = (row2 // DN_BLK) == (col2 // DN_BLK)
    lane_cl = lax.broadcasted_iota(jnp.int32, (c, LANES), 1)
    row_cl = lax.broadcasted_iota(jnp.int32, (c, LANES), 0)
    head0 = lane_cl < DN_HD

    def stack(x):
        return jnp.concatenate([jnp.where(head0, x, 0.0), jnp.where(head0, 0.0, x)], axis=0)

    def fold(x):
        return x[:c] + x[c:]

    def col_pair(x, lane0):
        return jnp.concatenate([x[:, lane0:lane0 + 1], x[:, lane0 + 1:lane0 + 2]], axis=0)

    bd_state = (lax.broadcasted_iota(jnp.int32, (LANES, LANES), 0) // DN_HD) == \
               (lax.broadcasted_iota(jnp.int32, (LANES, LANES), 1) // DN_HD)

    def phase1_chunk(n):
        rows = pl.ds(pl.multiple_of(n * c, c), c)
        gc = g_scr[rows, :]
        fwd_cum, bwd_cum = gc, gc
        sh = 1
        while sh < c:
            fwd_cum = fwd_cum + jnp.where(row_cl >= sh, pltpu.roll(fwd_cum, sh, 0), 0.0)
            bwd_cum = bwd_cum + jnp.where(row_cl < c - sh, pltpu.roll(bwd_cum, c - sh, 0), 0.0)
            sh *= 2
        cum = jnp.where(lane_cl < 6, fwd_cum, bwd_cum)
        cum_t = cum.T
        tot = jnp.where(lane_cl[0:1] < 6, cum[c - 1:c, :], cum[0:1, :])

        kc = k_scr[rows, :]
        qc = q_scr[rows, :]
        vc = v_scr[rows, :]
        k2 = stack(kc)
        q2 = stack(qc)
        v2 = stack(vc)
        k2b = k2.astype(BF16)
        kq = _dot_nt(jnp.concatenate([k2b, q2.astype(BF16)], axis=0), k2b)
        kk = kq[:PAIR]
        qk = kq[PAIR:]
        return dict(n=n, rows=rows, gc=gc, cum=cum, cum_t=cum_t, tot=tot, k2=k2, q2=q2, v2=v2, kk=kk, qk=qk)

    def phase1_chain(ch, d):
        gc, cum, cum_t, tot = ch["gc"], ch["cum"], ch["cum_t"], ch["tot"]
        beta2 = col_pair(gc, 2 * d)
        cum2 = col_pair(cum, 4 + 2 * d)
        cum_row = jnp.concatenate([cum_t[4 + 2 * d:5 + 2 * d, :], cum_t[5 + 2 * d:6 + 2 * d, :]], axis=1)
        tot2 = jnp.concatenate([jnp.broadcast_to(tot[:, 4 + 2 * d:5 + 2 * d], (c, 1)),
                                jnp.broadcast_to(tot[:, 5 + 2 * d:6 + 2 * d], (c, 1))], axis=0)
        if d == 0:
            incl, strict = same_head & (row2 >= col2), same_head & (row2 > col2)
        else:
            incl, strict = same_head & (row2 <= col2), same_head & (row2 < col2)
        decay = jnp.exp(jnp.where(incl, cum2 - cum_row, -jnp.inf))
        lmat = jnp.where(strict, beta2 * ch["kk"] * decay, 0.0)
        rhs = jnp.concatenate([ch["v2"] * beta2, ch["k2"] * (beta2 * jnp.exp(cum2))], axis=1)
        qk_d = jnp.where(incl, ch["qk"] * decay, 0.0) * (DN_HD ** -0.5)
        qd2 = ch["q2"] * ((DN_HD ** -0.5) * jnp.exp(cum2))
        kd = fold(ch["k2"] * jnp.exp(tot2 - cum2))
        dec = jnp.where(lane_cl[0:1] < DN_HD, jnp.exp(tot[:, 4 + 2 * d:5 + 2 * d]),
                        jnp.exp(tot[:, 5 + 2 * d:6 + 2 * d]))
        return dict(d=d, n=ch["n"], rows=ch["rows"], lmat=lmat, rhs=rhs, qk_d=qk_d, qd2=qd2, kd=kd, dec=dec)

    def phase1(i, carry):
        chunks = [phase1_chunk(i * DN_UNROLL + j) for j in range(DN_UNROLL)]
        chains = [phase1_chain(ch, d) for ch in chunks for d in range(2)]
        xs = [jnp.where(same_blk, -t["lmat"], 0.0) for t in chains]
        zs = [jnp.concatenate([jnp.where(same_blk, 0.0, t["lmat"]), t["rhs"]], axis=1) for t in chains]
        bdot = lambda a, b: _dot(a.astype(BF16), b.astype(BF16))

        def apply_powers(ms, vs, n_squarings):
            for _ in range(n_squarings):
                both = [bdot(m, jnp.concatenate([m, v], axis=1)) for m, v in zip(ms, vs)]
                vs = [v + b[:, PAIR:] for v, b in zip(vs, both)]
                ms = [b[:, :PAIR] for b in both]
            return [v + bdot(m, v) for m, v in zip(ms, vs)]

        zs = apply_powers(xs, zs, int(math.log2(DN_BLK)) - 1)
        ys = [-z[:, :PAIR] for z in zs]
        rs = apply_powers(ys, [z[:, PAIR:] for z in zs], int(math.log2(DN_CHUNK // DN_BLK)) - 1)
        a_s = [_dot(t["qk_d"].astype(BF16), sol.astype(BF16)) for t, sol in zip(chains, rs)]
        mns = [_dot_tn(t["kd"].astype(BF16), fold(sol).astype(BF16)) for t, sol in zip(chains, rs)]
        for t, a, mn in zip(chains, a_s, mns):
            d, n, rows = t["d"], t["n"], t["rows"]
            n_scr[d, n] = jnp.where(bd_state, mn[:, :LANES], 0.0)
            m_scr[d, n] = jnp.where(bd_state, -mn[:, LANES:], 0.0).astype(BF16)
            qe_scr[d, rows, :] = fold(t["qd2"] - a[:, LANES:]).astype(BF16)
            oi_scr[d, rows, :] = fold(a[:, :LANES])
            dec_scr[d, n] = jnp.broadcast_to(t["dec"], (SUBLANES, LANES))
        return carry

    lax.fori_loop(0, n_chunks // DN_UNROLL, phase1, 0)

    o_ref[...] = jnp.zeros((seq, LANES), F32)

    def phase2(i, states):
        new_states = []
        for d in range(2):
            n = i if d == 0 else n_chunks - 1 - i
            rows = pl.ds(pl.multiple_of(n * c, c), c)
            state = states[d]
            sb = state.astype(BF16)
            o_ref[rows, :] = o_ref[rows, :] + oi_scr[d, rows, :] + _dot(qe_scr[d, rows, :], sb)
            new_states.append(state * dec_scr[d, n][0:1, :] + _dot(m_scr[d, n], sb) + n_scr[d, n])
        return tuple(new_states)

    zero_state = jnp.zeros((LANES, LANES), F32)
    lax.fori_loop(0, n_chunks, phase2, (zero_state, zero_state))

    def finish(i, carry):
        base = pl.multiple_of(i * DN_ROWS, DN_ROWS)
        o = o_ref[pl.ds(base, DN_ROWS), :]
        ms = _group_sum64(o * o, ones_bd) * (1.0 / DN_HD)
        o_ref[pl.ds(base, DN_ROWS), :] = o * lax.rsqrt(ms + LN_EPS) * ng_ref[...] * _silu(z_ref[pl.ds(base, DN_ROWS), :])
        return carry

    lax.fori_loop(0, seq // DN_ROWS, finish, 0)


def _deltanet(proj3, dn_conv, prm, norm_g2):
    bsz, seq, _ = proj3.shape
    cq, ck, cv, cz, cg = (COL_DNQ // LANES, COL_DNK // LANES, COL_DNV // LANES, COL_DNZ // LANES, COL_GATE // LANES)
    col = lambda c0: pl.BlockSpec((None, seq, LANES), lambda b, hp: (b, 0, c0 + hp))
    cw = lambda c0: pl.BlockSpec((DN_CONV, LANES), lambda b, hp: (0, c0 + hp))
    n_chunks = seq // DN_CHUNK
    return pl.pallas_call(
        functools.partial(_dn_body, seq=seq),
        grid=(bsz, DN_HEADS // 2),
        in_specs=[col(cq), col(ck), col(cv), col(cz), col(cg), cw(0), cw(2), cw(4),
                  pl.BlockSpec((None, SUBLANES, LANES), lambda b, hp: (hp, 0, 0)),
                  pl.BlockSpec((1, LANES), lambda b, hp: (0, 0))],
        out_specs=pl.BlockSpec((None, seq, LANES), lambda b, hp: (b, 0, hp)),
        out_shape=jax.ShapeDtypeStruct((bsz, seq, DN_HEADS * DN_HD), F32),
        scratch_shapes=[pltpu.VMEM((seq + 2 * DN_PAD, LANES), F32),
                        pltpu.VMEM((seq, LANES), F32), pltpu.VMEM((seq, LANES), F32), pltpu.VMEM((seq, LANES), F32),
                        pltpu.VMEM((seq, LANES), F32),
                        pltpu.VMEM((2, n_chunks, SUBLANES, LANES), F32),
                        pltpu.VMEM((2, n_chunks, LANES, LANES), BF16),
                        pltpu.VMEM((2, n_chunks, LANES, LANES), F32),
                        pltpu.VMEM((2, seq, LANES), BF16),
                        pltpu.VMEM((2, seq, LANES), F32)],
        name="deltanet",
        compiler_params=_cparams(("parallel", "parallel")),
    )(proj3, proj3, proj3, proj3, proj3, dn_conv, dn_conv, dn_conv, prm, norm_g2)


def _outproj_body(yc_ref, yd_ref, yn_ref, x_ref, w_ref, g_ref, b_ref, o_ref):
    mix = jnp.concatenate([yc_ref[...], yd_ref[...], yn_ref[...]], axis=-1).astype(BF16)
    h = _dot(mix, w_ref[...])
    o_ref[...] = _layer_norm(DEEPNORM_ALPHA * x_ref[...] + h, g_ref[...], b_ref[...])


def _outproj_ln(yc, yd, yn, x2, w, g, b):
    t, d = x2.shape
    tm = min(512, t)
    row = lambda n: pl.BlockSpec((tm, n), lambda i: (i, 0))
    full = lambda shape: pl.BlockSpec(shape, lambda i: (0,) * len(shape))
    return pl.pallas_call(
        _outproj_body,
        grid=(t // tm,),
        in_specs=[row(yc.shape[1]), row(yd.shape[1]), row(yn.shape[1]), row(d), full(w.shape), full((1, d)), full((1, d))],
        out_specs=row(d),
        out_shape=jax.ShapeDtypeStruct((t, d), F32),
        name="outproj_ln",
        compiler_params=_cparams(("parallel",)),
    )(yc, yd, yn, x2, w, g, b)


def _ffn_body(x_ref, w1_ref, w3_ref, w2_ref, g_ref, b_ref, o_ref):
    x = x_ref[...]
    xb = x.astype(BF16)
    h = _silu(_dot(xb, w1_ref[...])) * _dot(xb, w3_ref[...])
    f = _dot(h.astype(BF16), w2_ref[...])
    o_ref[...] = _layer_norm(DEEPNORM_ALPHA * x + f, g_ref[...], b_ref[...])


def _ffn_ln(x2, w1, w3, w2, g, b):
    t, d = x2.shape
    f = w1.shape[1]
    tm = min(512, t)
    row = pl.BlockSpec((tm, d), lambda i: (i, 0))
    once = lambda shape: pl.BlockSpec(shape, lambda i: (0,) * len(shape), pipeline_mode=pl.Buffered(1))
    return pl.pallas_call(
        _ffn_body,
        grid=(t // tm,),
        in_specs=[row, once((d, f)), once((d, f)), once((f, d)), once((1, d)), once((1, d))],
        out_specs=row,
        out_shape=jax.ShapeDtypeStruct((t, d), F32),
        name="ffn_ln",
        compiler_params=_cparams(("parallel",)),
    )(x2, w1, w3, w2, g, b)


MOE_TB = 512
MOE_RUN = SUBLANES
MOE_BUF = 2 * MOE_TB + N_EXPERTS * MOE_RUN
META_P1, META_P2, META_G1, META_G2 = range(4)


def _moe_slots(t):
    n_blocks = t // min(MOE_TB, t)
    raw = 2 * t + n_blocks * N_EXPERTS * (MOE_RUN - 1) + N_EXPERTS * (MOE_TM - 1)
    return ((raw + MOE_TM - 1) // MOE_TM) * MOE_TM


def _router_body(x_ref, rw_ref, meta_ref, cnt_ref):
    tm = x_ref.shape[0]
    logits = jnp.dot(x_ref[...], rw_ref[...], preferred_element_type=F32, precision=lax.Precision.HIGHEST)
    lane = lax.broadcasted_iota(jnp.int32, logits.shape, 1)
    logits = jnp.where(lane < N_EXPERTS, logits, -jnp.inf)
    m1 = jnp.max(logits, axis=-1, keepdims=True)
    e1 = jnp.min(jnp.where(logits == m1, lane, LANES), axis=-1, keepdims=True)
    rest = jnp.where(lane == e1, -jnp.inf, logits)
    m2 = jnp.max(rest, axis=-1, keepdims=True)
    e2 = jnp.min(jnp.where(rest == m2, lane, LANES), axis=-1, keepdims=True)
    t = jnp.exp(m2 - m1)
    g1 = 1.0 / (1.0 + t)
    g2 = t / (1.0 + t)

    sel = jnp.where((lane == e1) | (lane == e2), 1.0, 0.0)
    r = lax.broadcasted_iota(jnp.int32, (tm, tm), 0)
    c = lax.broadcasted_iota(jnp.int32, (tm, tm), 1)
    strict_lower = jnp.where(r > c, 1.0, 0.0).astype(BF16)
    rank = _dot(strict_lower, sel.astype(BF16))
    cnt = jnp.sum(sel, axis=0, keepdims=True)
    run = jnp.floor((cnt + (MOE_RUN - 1.0)) * (1.0 / MOE_RUN)) * MOE_RUN
    lane1 = lane[0:1]
    start = run
    sh = 1
    while sh < N_EXPERTS:
        start = start + jnp.where(lane1 >= sh, pltpu.roll(start, sh, 1), 0.0)
        sh *= 2
    pos = rank + (start - run)
    p1 = jnp.sum(jnp.where(lane == e1, pos, 0.0), axis=-1, keepdims=True)
    p2 = jnp.sum(jnp.where(lane == e2, pos, 0.0), axis=-1, keepdims=True)
    cnt_ref[...] = jnp.broadcast_to(cnt, cnt_ref.shape)

    meta = jnp.where(lane == META_P1, p1, 0.0)
    meta = jnp.where(lane == META_P2, p2, meta)
    meta = jnp.where(lane == META_G1, g1, meta)
    meta = jnp.where(lane == META_G2, g2, meta)
    meta_ref[...] = meta


def _router(x2, rw_pad):
    t, d = x2.shape
    tm = min(MOE_TB, t)
    return pl.pallas_call(
        _router_body,
        grid=(t // tm,),
        in_specs=[pl.BlockSpec((tm, d), lambda i: (i, 0)), pl.BlockSpec((d, LANES), lambda i: (0, 0))],
        out_specs=[pl.BlockSpec((tm, LANES), lambda i: (i, 0)),
                   pl.BlockSpec((None, SUBLANES, LANES), lambda i: (i, 0, 0))],
        out_shape=[jax.ShapeDtypeStruct((t, LANES), F32),
                   jax.ShapeDtypeStruct((t // tm, SUBLANES, LANES), F32)],
        name="router",
        compiler_params=_cparams(("parallel",)),
    )(x2, rw_pad)


def _for_each_run_piece(tab_ref, n_runs, blk, fn):
    for e in range(N_EXPERTS):
        k = blk * N_EXPERTS + e
        lo, run, dst = tab_ref[k], tab_ref[n_runs + k], tab_ref[2 * n_runs + k]
        bit = MOE_TB
        while bit >= MOE_RUN:
            done = run & ~(2 * bit - 1)

            @pl.when((run & bit) != 0)
            def _():
                fn(pl.multiple_of(lo + done, MOE_RUN), pl.multiple_of(dst + done, MOE_RUN), bit)

            bit //= 2


def _dispatch_body(tab_ref, x_ref, meta_ref, xs_ref, buf, sem, *, n_runs):
    blk = pl.program_id(0)
    last = pl.num_programs(0) - 1
    cur = blk % 2
    tb = x_ref.shape[0]

    def copy(s, buf_row, sorted_row, n):
        return pltpu.make_async_copy(buf.at[s, pl.ds(buf_row, n), :], xs_ref.at[pl.ds(sorted_row, n), :], sem.at[s])

    @pl.when(blk == 0)
    def _():
        buf[0] = jnp.zeros((MOE_BUF, buf.shape[2]), F32)

        def tail(e):
            row = pl.multiple_of(tab_ref[3 * n_runs + e], MOE_TM)
            return pltpu.make_async_copy(buf.at[0, pl.ds(0, MOE_TM), :], xs_ref.at[pl.ds(row, MOE_TM), :], sem.at[0])

        for e in range(N_EXPERTS):
            pl.when(tab_ref[3 * n_runs + e] >= 0)(lambda e=e: tail(e).start())
        for e in range(N_EXPERTS):
            pl.when(tab_ref[3 * n_runs + e] >= 0)(lambda e=e: tail(e).wait())

        def unused_tile(j, carry):
            row = pl.multiple_of(j * MOE_TM, MOE_TM)
            c = pltpu.make_async_copy(buf.at[0, pl.ds(0, MOE_TM), :], xs_ref.at[pl.ds(row, MOE_TM), :], sem.at[0])
            c.start()
            c.wait()
            return carry

        lax.fori_loop(tab_ref[3 * n_runs + N_EXPERTS], xs_ref.shape[0] // MOE_TM, unused_tile, 0)

    meta_t = meta_ref[...].T
    p1 = meta_t[META_P1:META_P1 + 1, :]
    p2 = meta_t[META_P2:META_P2 + 1, :]
    slot = lax.broadcasted_iota(jnp.int32, (MOE_BUF, tb), 0).astype(F32)
    onehot = jnp.where((slot == p1) | (slot == p2), 1.0, 0.0).astype(BF16)
    buf[cur] = _dot(onehot, x_ref[...].astype(BF16))

    _for_each_run_piece(tab_ref, n_runs, blk, lambda *a: copy(cur, *a).start())

    @pl.when(blk > 0)
    def _():
        _for_each_run_piece(tab_ref, n_runs, blk - 1, lambda *a: copy(1 - cur, *a).wait())

    @pl.when(blk == last)
    def _():
        _for_each_run_piece(tab_ref, n_runs, blk, lambda *a: copy(cur, *a).wait())


def _dispatch(tab, x2, meta, n_slots):
    t, d = x2.shape
    tb = min(MOE_TB, t)
    grid_spec = pltpu.PrefetchScalarGridSpec(
        num_scalar_prefetch=1,
        grid=(t // tb,),
        in_specs=[pl.BlockSpec((tb, d), lambda i, tab: (i, 0)),
                  pl.BlockSpec((tb, LANES), lambda i, tab: (i, 0))],
        out_specs=pl.BlockSpec(memory_space=pl.ANY),
        scratch_shapes=[pltpu.VMEM((2, MOE_BUF, d), F32), pltpu.SemaphoreType.DMA((2,))],
    )
    return pl.pallas_call(
        functools.partial(_dispatch_body, n_runs=(t // tb) * N_EXPERTS),
        grid_spec=grid_spec,
        out_shape=jax.ShapeDtypeStruct((n_slots, d), F32),
        name="moe_dispatch",
        compiler_params=pltpu.CompilerParams(dimension_semantics=("arbitrary",), vmem_limit_bytes=VMEM_LIMIT,
                                             has_side_effects=True),
    )(tab, x2, meta)


def _experts_body(te_ref, nu_ref, xs_ref, w1_ref, w3_ref, w2_ref, ys_ref, xb_scr, acc_scr):
    i = pl.program_id(0)
    f = pl.program_id(1)
    used = i < nu_ref[0]

    @pl.when(used & (f == 0))
    def _():
        xb_scr[...] = xs_ref[...].astype(BF16)

    @pl.when(used)
    def _():
        xb = xb_scr[...]
        h = _silu(_dot(xb, w1_ref[...])) * _dot(xb, w3_ref[...])
        part = _dot(h.astype(BF16), w2_ref[...])

        @pl.when(f == 0)
        def _():
            acc_scr[...] = part

        @pl.when(f > 0)
        def _():
            acc_scr[...] = acc_scr[...] + part

    @pl.when(f == pl.num_programs(1) - 1)
    def _():
        @pl.when(used)
        def _():
            ys_ref[...] = acc_scr[...]

        @pl.when(jnp.logical_not(used))
        def _():
            ys_ref[...] = jnp.zeros_like(ys_ref)


def _experts(tile_expert, n_used, xs, w1, w3, w2):
    n_slots, d = xs.shape
    n_tiles = n_slots // MOE_TM
    nf = w1.shape[2] // MOE_TF

    def fidx(i, f, te, nu):
        return jnp.where(i < nu[0], f, nf - 1)

    grid_spec = pltpu.PrefetchScalarGridSpec(
        num_scalar_prefetch=2,
        grid=(n_tiles, nf),
        in_specs=[pl.BlockSpec((MOE_TM, d), lambda i, f, te, nu: (jnp.minimum(i, nu[0] - 1), 0)),
                  pl.BlockSpec((None, d, MOE_TF), lambda i, f, te, nu: (te[i], 0, fidx(i, f, te, nu))),
                  pl.BlockSpec((None, d, MOE_TF), lambda i, f, te, nu: (te[i], 0, fidx(i, f, te, nu))),
                  pl.BlockSpec((None, MOE_TF, d), lambda i, f, te, nu: (te[i], fidx(i, f, te, nu), 0))],
        out_specs=pl.BlockSpec((MOE_TM, d), lambda i, f, te, nu: (i, 0)),
        scratch_shapes=[pltpu.VMEM((MOE_TM, d), BF16), pltpu.VMEM((MOE_TM, d), F32)],
    )
    return pl.pallas_call(
        _experts_body,
        grid_spec=grid_spec,
        out_shape=jax.ShapeDtypeStruct((n_slots, d), F32),
        name="moe_experts",
        compiler_params=_cparams(("arbitrary", "arbitrary")),
    )(tile_expert, n_used, xs, w1, w3, w2)


def _split_bf16(v):
    hi = v.astype(BF16)
    return hi, (v - hi.astype(F32)).astype(BF16)


def _combine_body(tab_ref, x_ref, meta_ref, ys_ref, g_ref, b_ref, o_ref, ybuf, sem, *, n_runs):
    blk = pl.program_id(0)
    last = pl.num_programs(0) - 1
    cur = blk % 2
    tb = x_ref.shape[0]

    def copy(s, buf_row, sorted_row, n):
        return pltpu.make_async_copy(ys_ref.at[pl.ds(sorted_row, n), :], ybuf.at[s, pl.ds(buf_row, n), :], sem.at[s])

    @pl.when(blk == 0)
    def _():
        ybuf[...] = jnp.zeros_like(ybuf)
        _for_each_run_piece(tab_ref, n_runs, blk, lambda *a: copy(cur, *a).start())

    @pl.when(blk < last)
    def _():
        _for_each_run_piece(tab_ref, n_runs, blk + 1, lambda *a: copy(1 - cur, *a).start())

    _for_each_run_piece(tab_ref, n_runs, blk, lambda *a: copy(cur, *a).wait())

    meta = meta_ref[...]
    p1 = meta[:, META_P1:META_P1 + 1]
    p2 = meta[:, META_P2:META_P2 + 1]
    g1 = meta[:, META_G1:META_G1 + 1]
    g2 = meta[:, META_G2:META_G2 + 1]
    slot = lax.broadcasted_iota(jnp.int32, (tb, MOE_BUF), 1).astype(F32)
    weights = jnp.where(slot == p1, g1, 0.0) + jnp.where(slot == p2, g2, 0.0)
    wh, wl = _split_bf16(weights)
    yh, yl = _split_bf16(ybuf[cur])
    f = _dot(wh, yh) + _dot(wh, yl) + _dot(wl, yh)
    o_ref[...] = _layer_norm(DEEPNORM_ALPHA * x_ref[...] + f, g_ref[...], b_ref[...])


def _combine_ln(tab, x2, meta, ys, g, b):
    t, d = x2.shape
    tb = min(MOE_TB, t)
    grid_spec = pltpu.PrefetchScalarGridSpec(
        num_scalar_prefetch=1,
        grid=(t // tb,),
        in_specs=[pl.BlockSpec((tb, d), lambda i, tab: (i, 0)),
                  pl.BlockSpec((tb, LANES), lambda i, tab: (i, 0)),
                  pl.BlockSpec(memory_space=pl.ANY),
                  pl.BlockSpec((1, d), lambda i, tab: (0, 0)),
                  pl.BlockSpec((1, d), lambda i, tab: (0, 0))],
        out_specs=pl.BlockSpec((tb, d), lambda i, tab: (i, 0)),
        scratch_shapes=[pltpu.VMEM((2, MOE_BUF, d), F32), pltpu.SemaphoreType.DMA((2,))],
    )
    return pl.pallas_call(
        functools.partial(_combine_body, n_runs=(t // tb) * N_EXPERTS),
        grid_spec=grid_spec,
        out_shape=jax.ShapeDtypeStruct((t, d), F32),
        name="moe_combine_ln",
        compiler_params=_cparams(("arbitrary",)),
    )(tab, x2, meta, ys, g, b)


def _moe_ln(x2, rw_pad, w1, w3, w2, g, b):
    t, d = x2.shape
    meta, blk_cnt = _router(x2, rw_pad)
    cnt = blk_cnt[:, 0, :N_EXPERTS].astype(jnp.int32)
    run = ((cnt + MOE_RUN - 1) // MOE_RUN) * MOE_RUN
    lo = jnp.cumsum(run, axis=1) - run
    before = jnp.cumsum(run, axis=0) - run
    padded = ((jnp.sum(run, axis=0) + MOE_TM - 1) // MOE_TM) * MOE_TM
    ends = jnp.cumsum(padded)
    starts = ends - padded
    tails = jnp.where(padded > 0, ends - MOE_TM, -1)
    n_slots = _moe_slots(t)
    n_tiles = n_slots // MOE_TM
    n_used = (ends[-1] // MOE_TM).astype(jnp.int32)
    tab = jnp.concatenate([lo.reshape(-1), run.reshape(-1), (starts[None, :] + before).reshape(-1),
                           tails, n_used.reshape(1)]).astype(jnp.int32)
    tile_start = jnp.arange(n_tiles, dtype=jnp.int32) * MOE_TM
    tile_expert = jnp.sum((tile_start[:, None] >= ends[None, :]).astype(jnp.int32), axis=1)
    last_expert = jnp.sum((((n_used - 1) * MOE_TM) >= ends).astype(jnp.int32))
    tile_expert = jnp.where(jnp.arange(n_tiles) < n_used, tile_expert, last_expert).astype(jnp.int32)

    xs = _dispatch(tab, x2, meta, n_slots)
    ys = _experts(tile_expert, n_used.reshape(1), xs, w1, w3, w2)
    return _combine_ln(tab, x2, meta, ys, g, b)


def _rope_tables(seq):
    inv = ROPE_THETA ** (-jnp.arange(0, DIFF_HD, 2, dtype=F32) / DIFF_HD)
    ang = jnp.arange(seq, dtype=F32)[:, None] * inv[None, :]
    cos, sin = jnp.cos(ang), jnp.sin(ang)
    cos_t = jnp.tile(cos, (1, LANES // cos.shape[1]))
    sin_t = jnp.tile(jnp.concatenate([-sin, sin], axis=1), (1, LANES // (2 * sin.shape[1])))
    return cos_t, sin_t


def _gate_columns():
    cols = []
    for hp in range(DN_HEADS // 2):
        blk = [PROJ_RAW - 16 + d * DN_HEADS + 2 * hp + hl for d in range(2) for hl in range(2)]
        blk += [PROJ_RAW - 8 + d * DN_HEADS + 2 * hp + hl for d in range(2) for hl in range(2)]
        cols.append(blk)
    return cols


def _prep_w_in(w_in_l):
    parts = [w_in_l[:, :COL_GATE]]
    for blk in _gate_columns():
        parts.append(w_in_l[:, jnp.array(blk)])
        parts.append(jnp.zeros((w_in_l.shape[0], LANES - len(blk)), w_in_l.dtype))
    return jnp.concatenate(parts, axis=1).astype(BF16)


def _prep_dn_params(a_log_l, dt_bias_l):
    out = []
    for hp in range(DN_HEADS // 2):
        idx = [(d, 2 * hp + hl) for d in range(2) for hl in range(2)]
        a = jnp.stack([a_log_l[d, h] for d, h in idx])
        t = jnp.stack([dt_bias_l[d, h] for d, h in idx])
        blk = jnp.zeros((SUBLANES, LANES), F32)
        blk = blk.at[0, 4:8].set(a).at[1, 4:8].set(t)
        out.append(blk)
    return jnp.stack(out)


def kernel(x, w_in, w_o, ln1_g, ln1_b, ln2_g, ln2_b, conv_dw, conv_dw_b, conv_ln_g, conv_ln_b, conv_pw,
           diff_lambda, diff_subln_g, dn_conv, dn_a_log, dn_dt_bias, dn_norm_g,
           ffn_w1, ffn_w3, ffn_w2, router_w, moe_w1, moe_w3, moe_w2):
    bsz, seq, d = x.shape
    t = bsz * seq
    cos_t, sin_t = _rope_tables(seq)
    x2 = x.reshape(t, d)
    row = lambda v: v.reshape(1, -1)
    for layer in range(DEPTH):
        lambda_init = 0.8 - 0.6 * math.exp(-0.3 * layer)
        proj = _inproj(x2, _prep_w_in(w_in[layer]))
        proj3 = proj.reshape(bsz, seq, PROJ_PAD)
        y_conv = _conv_module(proj3, conv_dw[layer], row(conv_dw_b[layer]), row(conv_ln_g[layer]),
                              row(conv_ln_b[layer]), conv_pw[layer].astype(BF16))
        y_diff = _diff_attention(proj3, cos_t, sin_t, diff_lambda[layer], diff_subln_g[layer].reshape(-1, 1),
                                 lambda_init)
        y_dn = _deltanet(proj3, dn_conv[layer], _prep_dn_params(dn_a_log[layer], dn_dt_bias[layer]),
                         row(jnp.tile(dn_norm_g[layer], 2)))
        x2 = _outproj_ln(y_conv.reshape(t, -1), y_diff.reshape(t, -1), y_dn.reshape(t, -1), x2,
                         w_o[layer].astype(BF16), row(ln1_g[layer]), row(ln1_b[layer]))
        j = layer // 2
        if layer % 2 == 0:
            x2 = _ffn_ln(x2, ffn_w1[j].astype(BF16), ffn_w3[j].astype(BF16), ffn_w2[j].astype(BF16),
                         row(ln2_g[layer]), row(ln2_b[layer]))
        else:
            rw_pad = jnp.pad(router_w[j], ((0, 0), (0, LANES - N_EXPERTS)))
            x2 = _moe_ln(x2, rw_pad, moe_w1[j].astype(BF16), moe_w3[j].astype(BF16), moe_w2[j].astype(BF16),
                         row(ln2_g[layer]), row(ln2_b[layer]))
    return x2.reshape(bsz, seq, d)
```

```python
import functools
import math

import jax
import jax.numpy as jnp
from jax import lax
from jax.experimental import pallas as pl
from jax.experimental.pallas import tpu as pltpu

F32 = jnp.float32
BF16 = jnp.bfloat16

D_MODEL = 1024
DEPTH = 4
CONV_W = D_MODEL // 4
CONV_WIDTH = 31
DIFF_HEADS = 4
DIFF_HD = D_MODEL // 16
DIFF_VD = 2 * DIFF_HD
DN_HEADS = 4
DN_HD = D_MODEL // 16
DN_CONV = 5
DN_CHUNK = 64
ROPE_THETA = 10000.0
D_FF = 11 * D_MODEL // 4
N_EXPERTS = 8
D_FF_EXPERT = 7 * D_MODEL // 2
DEEPNORM_ALPHA = (2 * DEPTH) ** 0.25
LN_EPS = 1e-5

LANES = 128
SUBLANES = 8
VMEM_LIMIT = 56 * 2 ** 20

COL_CONV = 0
COL_Q = 512
COL_K = 1024
COL_V = 1536
COL_DNQ = 2048
COL_DNK = 2304
COL_DNV = 2560
COL_DNZ = 2816
COL_GATE = 3072
PROJ_RAW = 3088
PROJ_PAD = COL_GATE + 2 * LANES

MOE_TM = 512
MOE_TF = 1792


def _cparams(sem):
    return pltpu.CompilerParams(dimension_semantics=sem, vmem_limit_bytes=VMEM_LIMIT)


def _sigmoid(x):
    return 1.0 / (1.0 + jnp.exp(-x))


def _silu(x):
    return x * _sigmoid(x)


def _softplus(x):
    return jnp.maximum(x, 0.0) + jnp.log1p(jnp.exp(-jnp.abs(x)))


def _layer_norm(x, g, b):
    mu = jnp.mean(x, axis=-1, keepdims=True)
    xc = x - mu
    var = jnp.mean(xc * xc, axis=-1, keepdims=True)
    return xc * lax.rsqrt(var + LN_EPS) * g + b


def _dot(a, b):
    return jnp.dot(a, b, preferred_element_type=F32)


def _dot_nt(a, b):
    return lax.dot_general(a, b, (((1,), (1,)), ((), ())), preferred_element_type=F32)


def _dot_tn(a, b):
    return lax.dot_general(a, b, (((0,), (0,)), ((), ())), preferred_element_type=F32)


def _group_sum64(x, ones_bd):
    hi = x.astype(BF16)
    lo = (x - hi.astype(F32)).astype(BF16)
    return _dot(hi, ones_bd) + _dot(lo, ones_bd)


def _ones_blockdiag(n):
    r = lax.broadcasted_iota(jnp.int32, (n, n), 0) // DN_HD
    c = lax.broadcasted_iota(jnp.int32, (n, n), 1) // DN_HD
    return jnp.where(r == c, 1.0, 0.0).astype(BF16)


def _inproj_body(x_ref, w_ref, o_ref):
    o_ref[...] = _dot(x_ref[...].astype(BF16), w_ref[...])


def _inproj(x2, w):
    t, d = x2.shape
    n = w.shape[1]
    tm = min(512, t)
    return pl.pallas_call(
        _inproj_body,
        grid=(t // tm,),
        in_specs=[pl.BlockSpec((tm, d), lambda i: (i, 0)),
                  pl.BlockSpec((d, n), lambda i: (0, 0))],
        out_specs=pl.BlockSpec((tm, n), lambda i: (i, 0)),
        out_shape=jax.ShapeDtypeStruct((t, n), F32),
        name="inproj",
        compiler_params=_cparams(("parallel",)),
    )(x2, w)


CONV_PAD = 16
CONV_ROWS = 128


def _conv_body(p_ref, dw_ref, dwb_ref, g_ref, b_ref, pw_ref, o_ref, pad_scr, *, seq):
    zeros = jnp.zeros((CONV_PAD, CONV_W), F32)
    pad_scr[0:CONV_PAD, :] = zeros
    pad_scr[CONV_PAD + seq:2 * CONV_PAD + seq, :] = zeros

    def glu(i, carry):
        base = pl.multiple_of(i * CONV_ROWS, CONV_ROWS)
        p = p_ref[pl.ds(base, CONV_ROWS), :]
        pad_scr[pl.ds(base + CONV_PAD, CONV_ROWS), :] = p[:, :CONV_W] * _sigmoid(p[:, CONV_W:])
        return carry

    lax.fori_loop(0, seq // CONV_ROWS, glu, 0)
    half = (CONV_WIDTH - 1) // 2

    def conv(i, carry):
        base = pl.multiple_of(i * CONV_ROWS, CONV_ROWS)
        acc = jnp.zeros((CONV_ROWS, CONV_W), F32) + dwb_ref[...]
        win = pad_scr[pl.ds(base, CONV_ROWS + 2 * CONV_PAD), :]
        n_win = CONV_ROWS + 2 * CONV_PAD
        for r in range(SUBLANES):
            wr = win if r == 0 else pltpu.roll(win, n_win - r, 0)
            for j in range(CONV_WIDTH):
                off = CONV_PAD - half + j
                if off % SUBLANES == r:
                    acc = acc + wr[off - r:off - r + CONV_ROWS, :] * dw_ref[j:j + 1, :]
        y = _silu(_layer_norm(acc, g_ref[...], b_ref[...]))
        o_ref[pl.ds(base, CONV_ROWS), :] = _dot(y.astype(BF16), pw_ref[...])
        return carry

    lax.fori_loop(0, seq // CONV_ROWS, conv, 0)


def _conv_module(proj3, dw, dwb, g, b, pw):
    bsz, seq, _ = proj3.shape
    full = lambda shape: pl.BlockSpec(shape, lambda i: (0,) * len(shape))
    return pl.pallas_call(
        functools.partial(_conv_body, seq=seq),
        grid=(bsz,),
        in_specs=[pl.BlockSpec((None, seq, 2 * CONV_W), lambda i: (i, 0, COL_CONV // (2 * CONV_W))),
                  full((CONV_WIDTH, CONV_W)), full((1, CONV_W)), full((1, CONV_W)), full((1, CONV_W)),
                  full((CONV_W, CONV_W))],
        out_specs=pl.BlockSpec((None, seq, CONV_W), lambda i: (i, 0, 0)),
        out_shape=jax.ShapeDtypeStruct((bsz, seq, CONV_W), F32),
        scratch_shapes=[pltpu.VMEM((seq + 2 * CONV_PAD, CONV_W), F32)],
        name="conv_module",
        compiler_params=_cparams(("parallel",)),
    )(proj3, dw, dwb, g, b, pw)


ATTN_TQ = 2048
ATTN_SUB = 256
ATTN_KB = 512
ATTN_ONES = 16
ATTN_FOLD = 64


def _rope(x, cos, sin_signed):
    lane = lax.broadcasted_iota(jnp.int32, x.shape, 1)
    first = (lane % DIFF_HD) < (DIFF_HD // 2)
    rot = jnp.where(first, pltpu.roll(x, LANES - DIFF_HD // 2, 1), pltpu.roll(x, DIFF_HD // 2, 1))
    return x * cos + rot * sin_signed


def _attn_body(q_ref, k_ref, v_ref, cq_ref, sq_ref, ck_ref, sk_ref, dl_ref, g_ref, o_ref, kr_scr, vt_scr,
               *, lambda_init):
    @pl.when(pl.program_id(2) == 0)
    def _():
        kr_scr[...] = _rope(k_ref[...], ck_ref[...], sk_ref[...]).astype(BF16)
        vt_scr[0:DIFF_VD, :] = v_ref[...].T.astype(BF16)
        vt_scr[DIFF_VD:, :] = jnp.ones((ATTN_ONES, vt_scr.shape[1]), BF16)

    dl = dl_ref[...]
    lam = (jnp.exp(jnp.sum(dl[0:1] * dl[1:2], axis=-1, keepdims=True))
           - jnp.exp(jnp.sum(dl[2:3] * dl[3:4], axis=-1, keepdims=True)) + lambda_init)

    q = _rope(q_ref[...], cq_ref[...], sq_ref[...]) * (DIFF_HD ** -0.5 * math.log2(math.e))
    lane = lax.broadcasted_iota(jnp.int32, q.shape, 1)
    first_map = lane < DIFF_HD
    kr = kr_scr[...]
    vt = vt_scr[...]

    def fold_keys(x, op):
        part = op(x.reshape(x.shape[0] // ATTN_FOLD, ATTN_FOLD, x.shape[1]), axis=0)
        return op(part, axis=0, keepdims=True)

    n_sub = q.shape[0] // ATTN_SUB
    qms = [jnp.where(first_map if mp == 0 else jnp.logical_not(first_map), q, 0.0)[s * ATTN_SUB:(s + 1) * ATTN_SUB]
           for s in range(n_sub) for mp in range(2)]
    qbs = [qm.astype(BF16) for qm in qms]
    kb = min(ATTN_KB, kr.shape[0])
    m_run = acc = None
    for j in range(kr.shape[0] // kb):
        krj = kr[j * kb:(j + 1) * kb, :]
        vtj = vt[:, j * kb:(j + 1) * kb]
        sts = [_dot_nt(krj, qb) for qb in qbs]
        ms_ = [fold_keys(st, jnp.max) for st in sts]
        if j > 0:
            ms_ = [jnp.maximum(m, mo) for m, mo in zip(ms_, m_run)]
        ps = [jnp.exp2(st - m).astype(BF16) for st, m in zip(sts, ms_)]
        pv = [_dot(vtj, p) for p in ps]
        if j > 0:
            pv = [o + a * jnp.exp2(mo - m) for o, a, mo, m in zip(pv, acc, m_run, ms_)]
        m_run, acc = ms_, pv
    os_ = [o[0:DIFF_VD] / o[DIFF_VD:DIFF_VD + 1] for o in acc]
    for s in range(n_sub):
        ot = os_[2 * s] - lam * os_[2 * s + 1]
        ms = jnp.mean(ot * ot, axis=0, keepdims=True)
        o_ref[s * ATTN_SUB:(s + 1) * ATTN_SUB, :] = (ot * lax.rsqrt(ms + LN_EPS) * g_ref[...] * (1.0 - lambda_init)).T


def _diff_attention(proj3, cos_t, sin_t, diff_lambda, subln_g, lambda_init):
    bsz, seq, _ = proj3.shape
    tq = min(ATTN_TQ, seq)
    cq, ck, cv = COL_Q // LANES, COL_K // LANES, COL_V // LANES
    return pl.pallas_call(
        functools.partial(_attn_body, lambda_init=lambda_init),
        grid=(bsz, DIFF_HEADS, seq // tq),
        in_specs=[pl.BlockSpec((None, tq, LANES), lambda b, h, i: (b, i, cq + h)),
                  pl.BlockSpec((None, seq, LANES), lambda b, h, i: (b, 0, ck + h)),
                  pl.BlockSpec((None, seq, LANES), lambda b, h, i: (b, 0, cv + h)),
                  pl.BlockSpec((tq, LANES), lambda b, h, i: (i, 0)),
                  pl.BlockSpec((tq, LANES), lambda b, h, i: (i, 0)),
                  pl.BlockSpec((seq, LANES), lambda b, h, i: (0, 0)),
                  pl.BlockSpec((seq, LANES), lambda b, h, i: (0, 0)),
                  pl.BlockSpec((4, DIFF_HD), lambda b, h, i: (0, 0)),
                  pl.BlockSpec((DIFF_VD, 1), lambda b, h, i: (0, 0))],
        out_specs=pl.BlockSpec((None, tq, LANES), lambda b, h, i: (b, i, h)),
        out_shape=jax.ShapeDtypeStruct((bsz, seq, DIFF_HEADS * DIFF_VD), F32),
        scratch_shapes=[pltpu.VMEM((seq, LANES), BF16), pltpu.VMEM((DIFF_VD + ATTN_ONES, seq), BF16)],
        name="diff_attention",
        compiler_params=_cparams(("parallel", "parallel", "arbitrary")),
    )(proj3, proj3, proj3, cos_t, sin_t, cos_t, sin_t, diff_lambda, subln_g)


DN_PAD = 8
DN_ROWS = 256
DN_BLK = 16
PAIR = 2 * DN_CHUNK
DN_UNROLL = 4


def _dn_body(q_ref, k_ref, v_ref, z_ref, gt_ref, cwq_ref, cwk_ref, cwv_ref, prm_ref, ng_ref, o_ref,
             pad_scr, q_scr, k_scr, v_scr, g_scr, dec_scr, m_scr, n_scr, qe_scr, oi_scr, *prep_scr, seq):
    c = DN_CHUNK
    n_chunks = seq // c
    set_a, set_b = prep_scr[:len(prep_scr) // 2], prep_scr[len(prep_scr) // 2:]
    ones_bd = _ones_blockdiag(LANES)
    zeros = jnp.zeros((DN_PAD, LANES), F32)
    pad_scr[0:DN_PAD, :] = zeros
    pad_scr[DN_PAD + seq:2 * DN_PAD + seq, :] = zeros
    half = (DN_CONV - 1) // 2

    def conv_silu(src_ref, cw_ref, dst_scr, normalise):
        def copy(i, carry):
            base = pl.multiple_of(i * DN_ROWS, DN_ROWS)
            pad_scr[pl.ds(base + DN_PAD, DN_ROWS), :] = src_ref[pl.ds(base, DN_ROWS), :]
            return carry

        lax.fori_loop(0, seq // DN_ROWS, copy, 0)

        def conv(i, carry):
            base = pl.multiple_of(i * DN_ROWS, DN_ROWS)
            win = pad_scr[pl.ds(base, DN_ROWS + 2 * DN_PAD), :]
            acc = win[DN_PAD - half:DN_PAD - half + DN_ROWS, :] * cw_ref[0:1, :]
            for j in range(1, DN_CONV):
                off = DN_PAD - half + j
                acc = acc + win[off:off + DN_ROWS, :] * cw_ref[j:j + 1, :]
            y = _silu(acc)
            if normalise:
                y = y * lax.rsqrt(_group_sum64(y * y, ones_bd) + 1e-6)
            dst_scr[pl.ds(base, DN_ROWS), :] = y
            return carry

        lax.fori_loop(0, seq // DN_ROWS, conv, 0)

    conv_silu(q_ref, cwq_ref, q_scr, True)
    conv_silu(k_ref, cwk_ref, k_scr, True)
    conv_silu(v_ref, cwv_ref, v_scr, False)

    a_log = prm_ref[0:1, :]
    dt_bias = prm_ref[1:2, :]

    def gates(i, carry):
        base = pl.multiple_of(i * DN_ROWS, DN_ROWS)
        blk = gt_ref[pl.ds(base, DN_ROWS), :]
        lane = lax.broadcasted_iota(jnp.int32, blk.shape, 1)
        g_scr[pl.ds(base, DN_ROWS), :] = jnp.where(lane < 4, _sigmoid(blk),
                                                   -jnp.exp(a_log) * _softplus(blk + dt_bias))
        return carry

    lax.fori_loop(0, seq // DN_ROWS, gates, 0)

    row2 = lax.broadcasted_iota(jnp.int32, (PAIR, PAIR), 0)
    col2 = lax.broadcasted_iota(jnp.int32, (PAIR, PAIR), 1)
    same_head = (row2 // c) == (col2 // c)
    same_blk = (row2 // DN_BLK) == (col2 // DN_BLK)
    lane_cl = lax.broadcasted_iota(jnp.int32, (c, LANES), 1)
    row_cl = lax.broadcasted_iota(jnp.int32, (c, LANES), 0)
    head0 = lane_cl < DN_HD

    def stack(x):
        return jnp.concatenate([jnp.where(head0, x, 0.0), jnp.where(head0, 0.0, x)], axis=0)

    def fold(x):
        return x[:c] + x[c:]

    def col_pair(x, lane0):
        return jnp.concatenate([x[:, lane0:lane0 + 1], x[:, lane0 + 1:lane0 + 2]], axis=0)

    bd_state = (lax.broadcasted_iota(jnp.int32, (LANES, LANES), 0) // DN_HD) == \
               (lax.broadcasted_iota(jnp.int32, (LANES, LANES), 1) // DN_HD)

    def phase1_chunk(n):
        rows = pl.ds(pl.multiple_of(n * c, c), c)
        gc = g_scr[rows, :]
        fwd_cum, bwd_cum = gc, gc
        sh = 1
        while sh < c:
            fwd_cum = fwd_cum + jnp.where(row_cl >= sh, pltpu.roll(fwd_cum, sh, 0), 0.0)
            bwd_cum = bwd_cum + jnp.where(row_cl < c - sh, pltpu.roll(bwd_cum, c - sh, 0), 0.0)
            sh *= 2
        cum = jnp.where(lane_cl < 6, fwd_cum, bwd_cum)
        cum_t = cum.T
        tot = jnp.where(lane_cl[0:1] < 6, cum[c - 1:c, :], cum[0:1, :])

        kc = k_scr[rows, :]
        qc = q_scr[rows, :]
        vc = v_scr[rows, :]
        k2 = stack(kc)
        q2 = stack(qc)
        v2 = stack(vc)
        k2b = k2.astype(BF16)
        kq = _dot_nt(jnp.concatenate([k2b, q2.astype(BF16)], axis=0), k2b)
        kk = kq[:PAIR]
        qk = kq[PAIR:]
        return dict(n=n, rows=rows, gc=gc, cum=cum, cum_t=cum_t, tot=tot, k2=k2, q2=q2, v2=v2, kk=kk, qk=qk)

    def phase1_chain(ch, d):
        gc, cum, cum_t, tot = ch["gc"], ch["cum"], ch["cum_t"], ch["tot"]
        beta2 = col_pair(gc, 2 * d)
        cum2 = col_pair(cum, 4 + 2 * d)
        cum_row = jnp.concatenate([cum_t[4 + 2 * d:5 + 2 * d, :], cum_t[5 + 2 * d:6 + 2 * d, :]], axis=1)
        tot2 = jnp.concatenate([jnp.broadcast_to(tot[:, 4 + 2 * d:5 + 2 * d], (c, 1)),
                                jnp.broadcast_to(tot[:, 5 + 2 * d:6 + 2 * d], (c, 1))], axis=0)
        if d == 0:
            incl, strict = same_head & (row2 >= col2), same_head & (row2 > col2)
        else:
            incl, strict = same_head & (row2 <= col2), same_head & (row2 < col2)
        decay = jnp.exp(jnp.where(incl, cum2 - cum_row, -jnp.inf))
        lmat = jnp.where(strict, beta2 * ch["kk"] * decay, 0.0)
        rhs = jnp.concatenate([ch["v2"] * beta2, ch["k2"] * (beta2 * jnp.exp(cum2))], axis=1)
        qk_d = jnp.where(incl, ch["qk"] * decay, 0.0) * (DN_HD ** -0.5)
        qd2 = ch["q2"] * ((DN_HD ** -0.5) * jnp.exp(cum2))
        kd = fold(ch["k2"] * jnp.exp(tot2 - cum2))
        dec = jnp.where(lane_cl[0:1] < DN_HD, jnp.exp(tot[:, 4 + 2 * d:5 + 2 * d]),
                        jnp.exp(tot[:, 5 + 2 * d:6 + 2 * d]))
        return dict(d=d, n=ch["n"], rows=ch["rows"], lmat=lmat, rhs=rhs, qk_d=qk_d, qd2=qd2, kd=kd, dec=dec)

    unroll = min(DN_UNROLL, n_chunks // 2)
    n_chain = 2 * unroll

    def prepare(i, px, pz, pqk, pqd, pkd):
        chunks = [phase1_chunk(i * unroll + j) for j in range(unroll)]
        for idx, t in enumerate(phase1_chain(ch, d) for ch in chunks for d in range(2)):
            px[idx] = jnp.where(same_blk, -t["lmat"], 0.0).astype(BF16)
            pz[idx] = jnp.concatenate([jnp.where(same_blk, 0.0, t["lmat"]), t["rhs"]], axis=1)
            pqk[idx] = t["qk_d"].astype(BF16)
            pqd[idx] = fold(t["qd2"])
            pkd[idx] = t["kd"].astype(BF16)
            dec_scr[t["d"], t["n"]] = jnp.broadcast_to(t["dec"], (SUBLANES, LANES))

    def solve(i, px, pz, pqk, pqd, pkd):
        xs = [px[idx] for idx in range(n_chain)]
        zs = [pz[idx] for idx in range(n_chain)]
        bdot = lambda a, b: _dot(a.astype(BF16), b.astype(BF16))

        def apply_powers(ms, vs, n_squarings):
            for _ in range(n_squarings):
                both = [bdot(m, jnp.concatenate([m, v], axis=1)) for m, v in zip(ms, vs)]
                vs = [v + b[:, PAIR:] for v, b in zip(vs, both)]
                ms = [b[:, :PAIR] for b in both]
            return [v + bdot(m, v) for m, v in zip(ms, vs)]

        zs = apply_powers(xs, zs, int(math.log2(DN_BLK)) - 1)
        ys = [-z[:, :PAIR] for z in zs]
        rs = apply_powers(ys, [z[:, PAIR:] for z in zs], int(math.log2(DN_CHUNK // DN_BLK)) - 1)
        a_s = [_dot(pqk[idx], sol.astype(BF16)) for idx, sol in enumerate(rs)]
        mns = [_dot_tn(pkd[idx], fold(sol).astype(BF16)) for idx, sol in enumerate(rs)]
        for idx, (a, mn) in enumerate(zip(a_s, mns)):
            d, n = idx % 2, i * unroll + idx // 2
            rows = pl.ds(pl.multiple_of(n * c, c), c)
            n_scr[d, n] = jnp.where(bd_state, mn[:, :LANES], 0.0)
            m_scr[d, n] = jnp.where(bd_state, -mn[:, LANES:], 0.0).astype(BF16)
            qe_scr[d, rows, :] = (pqd[idx] - fold(a[:, LANES:])).astype(BF16)
            oi_scr[d, rows, :] = fold(a[:, :LANES])

    n_steps = n_chunks // unroll
    prepare(0, *set_a)

    def phase1(k, carry):
        solve(2 * k, *set_a)
        prepare(2 * k + 1, *set_b)
        solve(2 * k + 1, *set_b)
        prepare(jnp.minimum(2 * k + 2, n_steps - 1), *set_a)
        return carry

    lax.fori_loop(0, n_steps // 2, phase1, 0)

    o_ref[...] = jnp.zeros((seq, LANES), F32)

    def phase2(i, states):
        new_states = []
        for d in range(2):
            n = i if d == 0 else n_chunks - 1 - i
            rows = pl.ds(pl.multiple_of(n * c, c), c)
            state = states[d]
            sb = state.astype(BF16)
            o_ref[rows, :] = o_ref[rows, :] + oi_scr[d, rows, :] + _dot(qe_scr[d, rows, :], sb)
            new_states.append(state * dec_scr[d, n][0:1, :] + _dot(m_scr[d, n], sb) + n_scr[d, n])
        return tuple(new_states)

    zero_state = jnp.zeros((LANES, LANES), F32)
    lax.fori_loop(0, n_chunks, phase2, (zero_state, zero_state))

    def finish(i, carry):
        base = pl.multiple_of(i * DN_ROWS, DN_ROWS)
        o = o_ref[pl.ds(base, DN_ROWS), :]
        ms = _group_sum64(o * o, ones_bd) * (1.0 / DN_HD)
        o_ref[pl.ds(base, DN_ROWS), :] = o * lax.rsqrt(ms + LN_EPS) * ng_ref[...] * _silu(z_ref[pl.ds(base, DN_ROWS), :])
        return carry

    lax.fori_loop(0, seq // DN_ROWS, finish, 0)


def _deltanet(proj3, dn_conv, prm, norm_g2):
    bsz, seq, _ = proj3.shape
    cq, ck, cv, cz, cg = (COL_DNQ // LANES, COL_DNK // LANES, COL_DNV // LANES, COL_DNZ // LANES, COL_GATE // LANES)
    col = lambda c0: pl.BlockSpec((None, seq, LANES), lambda b, hp: (b, 0, c0 + hp))
    cw = lambda c0: pl.BlockSpec((DN_CONV, LANES), lambda b, hp: (0, c0 + hp))
    n_chunks = seq // DN_CHUNK
    n_chain = 2 * min(DN_UNROLL, n_chunks // 2)
    prep_set = [pltpu.VMEM((n_chain, PAIR, PAIR), BF16), pltpu.VMEM((n_chain, PAIR, PAIR + 2 * LANES), F32),
                pltpu.VMEM((n_chain, PAIR, PAIR), BF16), pltpu.VMEM((n_chain, DN_CHUNK, LANES), F32),
                pltpu.VMEM((n_chain, DN_CHUNK, LANES), BF16)]
    return pl.pallas_call(
        functools.partial(_dn_body, seq=seq),
        grid=(bsz, DN_HEADS // 2),
        in_specs=[col(cq), col(ck), col(cv), col(cz), col(cg), cw(0), cw(2), cw(4),
                  pl.BlockSpec((None, SUBLANES, LANES), lambda b, hp: (hp, 0, 0)),
                  pl.BlockSpec((1, LANES), lambda b, hp: (0, 0))],
        out_specs=pl.BlockSpec((None, seq, LANES), lambda b, hp: (b, 0, hp)),
        out_shape=jax.ShapeDtypeStruct((bsz, seq, DN_HEADS * DN_HD), F32),
        scratch_shapes=[pltpu.VMEM((seq + 2 * DN_PAD, LANES), F32),
                        pltpu.VMEM((seq, LANES), F32), pltpu.VMEM((seq, LANES), F32), pltpu.VMEM((seq, LANES), F32),
                        pltpu.VMEM((seq, LANES), F32),
                        pltpu.VMEM((2, n_chunks, SUBLANES, LANES), F32),
                        pltpu.VMEM((2, n_chunks, LANES, LANES), BF16),
                        pltpu.VMEM((2, n_chunks, LANES, LANES), F32),
                        pltpu.VMEM((2, seq, LANES), BF16),
                        pltpu.VMEM((2, seq, LANES), F32)] + prep_set + prep_set,
        name="deltanet",
        compiler_params=_cparams(("parallel", "parallel")),
    )(proj3, proj3, proj3, proj3, proj3, dn_conv, dn_conv, dn_conv, prm, norm_g2)


def _outproj_body(yc_ref, yd_ref, yn_ref, x_ref, w_ref, g_ref, b_ref, o_ref):
    mix = jnp.concatenate([yc_ref[...], yd_ref[...], yn_ref[...]], axis=-1).astype(BF16)
    h = _dot(mix, w_ref[...])
    o_ref[...] = _layer_norm(DEEPNORM_ALPHA * x_ref[...] + h, g_ref[...], b_ref[...])


def _outproj_ln(yc, yd, yn, x2, w, g, b):
    t, d = x2.shape
    tm = min(512, t)
    row = lambda n: pl.BlockSpec((tm, n), lambda i: (i, 0))
    full = lambda shape: pl.BlockSpec(shape, lambda i: (0,) * len(shape))
    return pl.pallas_call(
        _outproj_body,
        grid=(t // tm,),
        in_specs=[row(yc.shape[1]), row(yd.shape[1]), row(yn.shape[1]), row(d), full(w.shape), full((1, d)), full((1, d))],
        out_specs=row(d),
        out_shape=jax.ShapeDtypeStruct((t, d), F32),
        name="outproj_ln",
        compiler_params=_cparams(("parallel",)),
    )(yc, yd, yn, x2, w, g, b)


def _ffn_body(x_ref, w1_ref, w3_ref, w2_ref, g_ref, b_ref, o_ref):
    x = x_ref[...]
    xb = x.astype(BF16)
    h = _silu(_dot(xb, w1_ref[...])) * _dot(xb, w3_ref[...])
    f = _dot(h.astype(BF16), w2_ref[...])
    o_ref[...] = _layer_norm(DEEPNORM_ALPHA * x + f, g_ref[...], b_ref[...])


def _ffn_ln(x2, w1, w3, w2, g, b):
    t, d = x2.shape
    f = w1.shape[1]
    tm = min(512, t)
    row = pl.BlockSpec((tm, d), lambda i: (i, 0))
    once = lambda shape: pl.BlockSpec(shape, lambda i: (0,) * len(shape), pipeline_mode=pl.Buffered(1))
    return pl.pallas_call(
        _ffn_body,
        grid=(t // tm,),
        in_specs=[row, once((d, f)), once((d, f)), once((f, d)), once((1, d)), once((1, d))],
        out_specs=row,
        out_shape=jax.ShapeDtypeStruct((t, d), F32),
        name="ffn_ln",
        compiler_params=_cparams(("parallel",)),
    )(x2, w1, w3, w2, g, b)


MOE_TB = 512
MOE_RUN = SUBLANES
MOE_BUF = 2 * MOE_TB + N_EXPERTS * MOE_RUN
META_P1, META_P2, META_G1, META_G2 = range(4)


def _moe_slots(t):
    n_blocks = t // min(MOE_TB, t)
    raw = 2 * t + n_blocks * N_EXPERTS * (MOE_RUN - 1) + N_EXPERTS * (MOE_TM - 1)
    return ((raw + MOE_TM - 1) // MOE_TM) * MOE_TM


def _router_body(x_ref, rw_ref, meta_ref, cnt_ref):
    tm = x_ref.shape[0]
    logits = jnp.dot(x_ref[...], rw_ref[...], preferred_element_type=F32, precision=lax.Precision.HIGHEST)
    lane = lax.broadcasted_iota(jnp.int32, logits.shape, 1)
    logits = jnp.where(lane < N_EXPERTS, logits, -jnp.inf)
    m1 = jnp.max(logits, axis=-1, keepdims=True)
    e1 = jnp.min(jnp.where(logits == m1, lane, LANES), axis=-1, keepdims=True)
    rest = jnp.where(lane == e1, -jnp.inf, logits)
    m2 = jnp.max(rest, axis=-1, keepdims=True)
    e2 = jnp.min(jnp.where(rest == m2, lane, LANES), axis=-1, keepdims=True)
    t = jnp.exp(m2 - m1)
    g1 = 1.0 / (1.0 + t)
    g2 = t / (1.0 + t)

    sel = jnp.where((lane == e1) | (lane == e2), 1.0, 0.0)
    r = lax.broadcasted_iota(jnp.int32, (tm, tm), 0)
    c = lax.broadcasted_iota(jnp.int32, (tm, tm), 1)
    strict_lower = jnp.where(r > c, 1.0, 0.0).astype(BF16)
    rank = _dot(strict_lower, sel.astype(BF16))
    cnt = jnp.sum(sel, axis=0, keepdims=True)
    run = jnp.floor((cnt + (MOE_RUN - 1.0)) * (1.0 / MOE_RUN)) * MOE_RUN
    lane1 = lane[0:1]
    start = run
    sh = 1
    while sh < N_EXPERTS:
        start = start + jnp.where(lane1 >= sh, pltpu.roll(start, sh, 1), 0.0)
        sh *= 2
    pos = rank + (start - run)
    p1 = jnp.sum(jnp.where(lane == e1, pos, 0.0), axis=-1, keepdims=True)
    p2 = jnp.sum(jnp.where(lane == e2, pos, 0.0), axis=-1, keepdims=True)
    cnt_ref[...] = jnp.broadcast_to(cnt, cnt_ref.shape)

    meta = jnp.where(lane == META_P1, p1, 0.0)
    meta = jnp.where(lane == META_P2, p2, meta)
    meta = jnp.where(lane == META_G1, g1, meta)
    meta = jnp.where(lane == META_G2, g2, meta)
    meta_ref[...] = meta


def _router(x2, rw_pad):
    t, d = x2.shape
    tm = min(MOE_TB, t)
    return pl.pallas_call(
        _router_body,
        grid=(t // tm,),
        in_specs=[pl.BlockSpec((tm, d), lambda i: (i, 0)), pl.BlockSpec((d, LANES), lambda i: (0, 0))],
        out_specs=[pl.BlockSpec((tm, LANES), lambda i: (i, 0)),
                   pl.BlockSpec((None, SUBLANES, LANES), lambda i: (i, 0, 0))],
        out_shape=[jax.ShapeDtypeStruct((t, LANES), F32),
                   jax.ShapeDtypeStruct((t // tm, SUBLANES, LANES), F32)],
        name="router",
        compiler_params=_cparams(("parallel",)),
    )(x2, rw_pad)


def _for_each_run_piece(tab_ref, n_runs, blk, fn):
    for e in range(N_EXPERTS):
        k = blk * N_EXPERTS + e
        lo, run, dst = tab_ref[k], tab_ref[n_runs + k], tab_ref[2 * n_runs + k]
        bit = MOE_TB
        while bit >= MOE_RUN:
            done = run & ~(2 * bit - 1)

            @pl.when((run & bit) != 0)
            def _():
                fn(pl.multiple_of(lo + done, MOE_RUN), pl.multiple_of(dst + done, MOE_RUN), bit)

            bit //= 2


def _dispatch_body(tab_ref, x_ref, meta_ref, xs_ref, buf, sem, *, n_runs):
    blk = pl.program_id(0)
    last = pl.num_programs(0) - 1
    cur = blk % 2
    tb = x_ref.shape[0]

    def copy(s, buf_row, sorted_row, n):
        return pltpu.make_async_copy(buf.at[s, pl.ds(buf_row, n), :], xs_ref.at[pl.ds(sorted_row, n), :], sem.at[s])

    @pl.when(blk == 0)
    def _():
        buf[0] = jnp.zeros((MOE_BUF, buf.shape[2]), F32)

        def tail(e):
            row = pl.multiple_of(tab_ref[3 * n_runs + e], MOE_TM)
            return pltpu.make_async_copy(buf.at[0, pl.ds(0, MOE_TM), :], xs_ref.at[pl.ds(row, MOE_TM), :], sem.at[0])

        for e in range(N_EXPERTS):
            pl.when(tab_ref[3 * n_runs + e] >= 0)(lambda e=e: tail(e).start())
        for e in range(N_EXPERTS):
            pl.when(tab_ref[3 * n_runs + e] >= 0)(lambda e=e: tail(e).wait())

        def unused_tile(j, carry):
            row = pl.multiple_of(j * MOE_TM, MOE_TM)
            c = pltpu.make_async_copy(buf.at[0, pl.ds(0, MOE_TM), :], xs_ref.at[pl.ds(row, MOE_TM), :], sem.at[0])
            c.start()
            c.wait()
            return carry

        lax.fori_loop(tab_ref[3 * n_runs + N_EXPERTS], xs_ref.shape[0] // MOE_TM, unused_tile, 0)

    meta_t = meta_ref[...].T
    p1 = meta_t[META_P1:META_P1 + 1, :]
    p2 = meta_t[META_P2:META_P2 + 1, :]
    slot = lax.broadcasted_iota(jnp.int32, (MOE_BUF, tb), 0).astype(F32)
    onehot = jnp.where((slot == p1) | (slot == p2), 1.0, 0.0).astype(BF16)
    buf[cur] = _dot(onehot, x_ref[...].astype(BF16))

    _for_each_run_piece(tab_ref, n_runs, blk, lambda *a: copy(cur, *a).start())

    @pl.when(blk > 0)
    def _():
        _for_each_run_piece(tab_ref, n_runs, blk - 1, lambda *a: copy(1 - cur, *a).wait())

    @pl.when(blk == last)
    def _():
        _for_each_run_piece(tab_ref, n_runs, blk, lambda *a: copy(cur, *a).wait())


def _dispatch(tab, x2, meta, n_slots):
    t, d = x2.shape
    tb = min(MOE_TB, t)
    grid_spec = pltpu.PrefetchScalarGridSpec(
        num_scalar_prefetch=1,
        grid=(t // tb,),
        in_specs=[pl.BlockSpec((tb, d), lambda i, tab: (i, 0)),
                  pl.BlockSpec((tb, LANES), lambda i, tab: (i, 0))],
        out_specs=pl.BlockSpec(memory_space=pl.ANY),
        scratch_shapes=[pltpu.VMEM((2, MOE_BUF, d), F32), pltpu.SemaphoreType.DMA((2,))],
    )
    return pl.pallas_call(
        functools.partial(_dispatch_body, n_runs=(t // tb) * N_EXPERTS),
        grid_spec=grid_spec,
        out_shape=jax.ShapeDtypeStruct((n_slots, d), F32),
        name="moe_dispatch",
        compiler_params=pltpu.CompilerParams(dimension_semantics=("arbitrary",), vmem_limit_bytes=VMEM_LIMIT,
                                             has_side_effects=True),
    )(tab, x2, meta)


def _experts_body(te_ref, nu_ref, xs_ref, w1_ref, w3_ref, w2_ref, ys_ref, xb_scr, acc_scr):
    i = pl.program_id(0)
    f = pl.program_id(1)
    used = i < nu_ref[0]

    @pl.when(used & (f == 0))
    def _():
        xb_scr[...] = xs_ref[...].astype(BF16)

    @pl.when(used)
    def _():
        xb = xb_scr[...]
        h = _silu(_dot(xb, w1_ref[...])) * _dot(xb, w3_ref[...])
        part = _dot(h.astype(BF16), w2_ref[...])

        @pl.when(f == 0)
        def _():
            acc_scr[...] = part

        @pl.when(f > 0)
        def _():
            acc_scr[...] = acc_scr[...] + part

    @pl.when(f == pl.num_programs(1) - 1)
    def _():
        @pl.when(used)
        def _():
            ys_ref[...] = acc_scr[...]

        @pl.when(jnp.logical_not(used))
        def _():
            ys_ref[...] = jnp.zeros_like(ys_ref)


def _experts(tile_expert, n_used, xs, w1, w3, w2):
    n_slots, d = xs.shape
    n_tiles = n_slots // MOE_TM
    nf = w1.shape[2] // MOE_TF

    def fidx(i, f, te, nu):
        return jnp.where(i < nu[0], f, nf - 1)

    grid_spec = pltpu.PrefetchScalarGridSpec(
        num_scalar_prefetch=2,
        grid=(n_tiles, nf),
        in_specs=[pl.BlockSpec((MOE_TM, d), lambda i, f, te, nu: (jnp.minimum(i, nu[0] - 1), 0)),
                  pl.BlockSpec((None, d, MOE_TF), lambda i, f, te, nu: (te[i], 0, fidx(i, f, te, nu))),
                  pl.BlockSpec((None, d, MOE_TF), lambda i, f, te, nu: (te[i], 0, fidx(i, f, te, nu))),
                  pl.BlockSpec((None, MOE_TF, d), lambda i, f, te, nu: (te[i], fidx(i, f, te, nu), 0))],
        out_specs=pl.BlockSpec((MOE_TM, d), lambda i, f, te, nu: (i, 0)),
        scratch_shapes=[pltpu.VMEM((MOE_TM, d), BF16), pltpu.VMEM((MOE_TM, d), F32)],
    )
    return pl.pallas_call(
        _experts_body,
        grid_spec=grid_spec,
        out_shape=jax.ShapeDtypeStruct((n_slots, d), F32),
        name="moe_experts",
        compiler_params=_cparams(("arbitrary", "arbitrary")),
    )(tile_expert, n_used, xs, w1, w3, w2)


def _split_bf16(v):
    hi = v.astype(BF16)
    return hi, (v - hi.astype(F32)).astype(BF16)


def _combine_body(tab_ref, x_ref, meta_ref, ys_ref, g_ref, b_ref, o_ref, ybuf, sem, *, n_runs):
    blk = pl.program_id(0)
    last = pl.num_programs(0) - 1
    cur = blk % 2
    tb = x_ref.shape[0]

    def copy(s, buf_row, sorted_row, n):
        return pltpu.make_async_copy(ys_ref.at[pl.ds(sorted_row, n), :], ybuf.at[s, pl.ds(buf_row, n), :], sem.at[s])

    @pl.when(blk == 0)
    def _():
        ybuf[...] = jnp.zeros_like(ybuf)
        _for_each_run_piece(tab_ref, n_runs, blk, lambda *a: copy(cur, *a).start())

    @pl.when(blk < last)
    def _():
        _for_each_run_piece(tab_ref, n_runs, blk + 1, lambda *a: copy(1 - cur, *a).start())

    _for_each_run_piece(tab_ref, n_runs, blk, lambda *a: copy(cur, *a).wait())

    meta = meta_ref[...]
    p1 = meta[:, META_P1:META_P1 + 1]
    p2 = meta[:, META_P2:META_P2 + 1]
    g1 = meta[:, META_G1:META_G1 + 1]
    g2 = meta[:, META_G2:META_G2 + 1]
    slot = lax.broadcasted_iota(jnp.int32, (tb, MOE_BUF), 1).astype(F32)
    weights = jnp.where(slot == p1, g1, 0.0) + jnp.where(slot == p2, g2, 0.0)
    wh, wl = _split_bf16(weights)
    yh, yl = _split_bf16(ybuf[cur])
    f = _dot(wh, yh) + _dot(wh, yl) + _dot(wl, yh)
    o_ref[...] = _layer_norm(DEEPNORM_ALPHA * x_ref[...] + f, g_ref[...], b_ref[...])


def _combine_ln(tab, x2, meta, ys, g, b):
    t, d = x2.shape
    tb = min(MOE_TB, t)
    grid_spec = pltpu.PrefetchScalarGridSpec(
        num_scalar_prefetch=1,
        grid=(t // tb,),
        in_specs=[pl.BlockSpec((tb, d), lambda i, tab: (i, 0)),
                  pl.BlockSpec((tb, LANES), lambda i, tab: (i, 0)),
                  pl.BlockSpec(memory_space=pl.ANY),
                  pl.BlockSpec((1, d), lambda i, tab: (0, 0)),
                  pl.BlockSpec((1, d), lambda i, tab: (0, 0))],
        out_specs=pl.BlockSpec((tb, d), lambda i, tab: (i, 0)),
        scratch_shapes=[pltpu.VMEM((2, MOE_BUF, d), F32), pltpu.SemaphoreType.DMA((2,))],
    )
    return pl.pallas_call(
        functools.partial(_combine_body, n_runs=(t // tb) * N_EXPERTS),
        grid_spec=grid_spec,
        out_shape=jax.ShapeDtypeStruct((t, d), F32),
        name="moe_combine_ln",
        compiler_params=_cparams(("arbitrary",)),
    )(tab, x2, meta, ys, g, b)


def _moe_ln(x2, rw_pad, w1, w3, w2, g, b):
    t, d = x2.shape
    meta, blk_cnt = _router(x2, rw_pad)
    cnt = blk_cnt[:, 0, :N_EXPERTS].astype(jnp.int32)
    run = ((cnt + MOE_RUN - 1) // MOE_RUN) * MOE_RUN
    lo = jnp.cumsum(run, axis=1) - run
    before = jnp.cumsum(run, axis=0) - run
    padded = ((jnp.sum(run, axis=0) + MOE_TM - 1) // MOE_TM) * MOE_TM
    ends = jnp.cumsum(padded)
    starts = ends - padded
    tails = jnp.where(padded > 0, ends - MOE_TM, -1)
    n_slots = _moe_slots(t)
    n_tiles = n_slots // MOE_TM
    n_used = (ends[-1] // MOE_TM).astype(jnp.int32)
    tab = jnp.concatenate([lo.reshape(-1), run.reshape(-1), (starts[None, :] + before).reshape(-1),
                           tails, n_used.reshape(1)]).astype(jnp.int32)
    tile_start = jnp.arange(n_tiles, dtype=jnp.int32) * MOE_TM
    tile_expert = jnp.sum((tile_start[:, None] >= ends[None, :]).astype(jnp.int32), axis=1)
    last_expert = jnp.sum((((n_used - 1) * MOE_TM) >= ends).astype(jnp.int32))
    tile_expert = jnp.where(jnp.arange(n_tiles) < n_used, tile_expert, last_expert).astype(jnp.int32)

    xs = _dispatch(tab, x2, meta, n_slots)
    ys = _experts(tile_expert, n_used.reshape(1), xs, w1, w3, w2)
    return _combine_ln(tab, x2, meta, ys, g, b)


def _rope_tables(seq):
    inv = ROPE_THETA ** (-jnp.arange(0, DIFF_HD, 2, dtype=F32) / DIFF_HD)
    ang = jnp.arange(seq, dtype=F32)[:, None] * inv[None, :]
    cos, sin = jnp.cos(ang), jnp.sin(ang)
    cos_t = jnp.tile(cos, (1, LANES // cos.shape[1]))
    sin_t = jnp.tile(jnp.concatenate([-sin, sin], axis=1), (1, LANES // (2 * sin.shape[1])))
    return cos_t, sin_t


def _gate_columns():
    cols = []
    for hp in range(DN_HEADS // 2):
        blk = [PROJ_RAW - 16 + d * DN_HEADS + 2 * hp + hl for d in range(2) for hl in range(2)]
        blk += [PROJ_RAW - 8 + d * DN_HEADS + 2 * hp + hl for d in range(2) for hl in range(2)]
        cols.append(blk)
    return cols


def _prep_w_in(w_in_l):
    parts = [w_in_l[:, :COL_GATE]]
    for blk in _gate_columns():
        parts.append(w_in_l[:, jnp.array(blk)])
        parts.append(jnp.zeros((w_in_l.shape[0], LANES - len(blk)), w_in_l.dtype))
    return jnp.concatenate(parts, axis=1).astype(BF16)


def _prep_dn_params(a_log_l, dt_bias_l):
    out = []
    for hp in range(DN_HEADS // 2):
        idx = [(d, 2 * hp + hl) for d in range(2) for hl in range(2)]
        a = jnp.stack([a_log_l[d, h] for d, h in idx])
        t = jnp.stack([dt_bias_l[d, h] for d, h in idx])
        blk = jnp.zeros((SUBLANES, LANES), F32)
        blk = blk.at[0, 4:8].set(a).at[1, 4:8].set(t)
        out.append(blk)
    return jnp.stack(out)


def kernel(x, w_in, w_o, ln1_g, ln1_b, ln2_g, ln2_b, conv_dw, conv_dw_b, conv_ln_g, conv_ln_b, conv_pw,
           diff_lambda, diff_subln_g, dn_conv, dn_a_log, dn_dt_bias, dn_norm_g,
           ffn_w1, ffn_w3, ffn_w2, router_w, moe_w1, moe_w3, moe_w2):
    bsz, seq, d = x.shape
    t = bsz * seq
    cos_t, sin_t = _rope_tables(seq)
    x2 = x.reshape(t, d)
    row = lambda v: v.reshape(1, -1)
    for layer in range(DEPTH):
        lambda_init = 0.8 - 0.6 * math.exp(-0.3 * layer)
        proj = _inproj(x2, _prep_w_in(w_in[layer]))
        proj3 = proj.reshape(bsz, seq, PROJ_PAD)
        y_conv = _conv_module(proj3, conv_dw[layer], row(conv_dw_b[layer]), row(conv_ln_g[layer]),
                              row(conv_ln_b[layer]), conv_pw[layer].astype(BF16))
        y_diff = _diff_attention(proj3, cos_t, sin_t, diff_lambda[layer], diff_subln_g[layer].reshape(-1, 1),
                                 lambda_init)
        y_dn = _deltanet(proj3, dn_conv[layer], _prep_dn_params(dn_a_log[layer], dn_dt_bias[layer]),
                         row(jnp.tile(dn_norm_g[layer], 2)))
        x2 = _outproj_ln(y_conv.reshape(t, -1), y_diff.reshape(t, -1), y_dn.reshape(t, -1), x2,
                         w_o[layer].astype(BF16), row(ln1_g[layer]), row(ln1_b[layer]))
        j = layer // 2
        if layer % 2 == 0:
            x2 = _ffn_ln(x2, ffn_w1[j].astype(BF16), ffn_w3[j].astype(BF16), ffn_w2[j].astype(BF16),
                         row(ln2_g[layer]), row(ln2_b[layer]))
        else:
            rw_pad = jnp.pad(router_w[j], ((0, 0), (0, LANES - N_EXPERTS)))
            x2 = _moe_ln(x2, rw_pad, moe_w1[j].astype(BF16), moe_w3[j].astype(BF16), moe_w2[j].astype(BF16),
                         row(ln2_g[layer]), row(ln2_b[layer]))
    return x2.reshape(bsz, seq, d)
```

```python
import functools
import math

import jax
import jax.numpy as jnp
from jax import lax
from jax.experimental import pallas as pl
from jax.experimental.pallas import tpu as pltpu

F32 = jnp.float32
BF16 = jnp.bfloat16

D_MODEL = 1024
DEPTH = 4
CONV_W = D_MODEL // 4
CONV_WIDTH = 31
DIFF_HEADS = 4
DIFF_HD = D_MODEL // 16
DIFF_VD = 2 * DIFF_HD
DN_HEADS = 4
DN_HD = D_MODEL // 16
DN_CONV = 5
DN_CHUNK = 64
ROPE_THETA = 10000.0
D_FF = 11 * D_MODEL // 4
N_EXPERTS = 8
D_FF_EXPERT = 7 * D_MODEL // 2
DEEPNORM_ALPHA = (2 * DEPTH) ** 0.25
LN_EPS = 1e-5

LANES = 128
SUBLANES = 8
VMEM_LIMIT = 56 * 2 ** 20

COL_CONV = 0
COL_Q = 512
COL_K = 1024
COL_V = 1536
COL_DNQ = 2048
COL_DNK = 2304
COL_DNV = 2560
COL_DNZ = 2816
COL_GATE = 3072
PROJ_RAW = 3088
PROJ_PAD = COL_GATE + 2 * LANES

MOE_TM = 512
MOE_TF = 1792


def _cparams(sem):
    return pltpu.CompilerParams(dimension_semantics=sem, vmem_limit_bytes=VMEM_LIMIT)


def _sigmoid(x):
    return 1.0 / (1.0 + jnp.exp(-x))


def _silu(x):
    return x * _sigmoid(x)


def _softplus(x):
    return jnp.maximum(x, 0.0) + jnp.log1p(jnp.exp(-jnp.abs(x)))


def _layer_norm(x, g, b):
    mu = jnp.mean(x, axis=-1, keepdims=True)
    xc = x - mu
    var = jnp.mean(xc * xc, axis=-1, keepdims=True)
    return xc * lax.rsqrt(var + LN_EPS) * g + b


def _dot(a, b):
    return jnp.dot(a, b, preferred_element_type=F32)


def _dot_nt(a, b):
    return lax.dot_general(a, b, (((1,), (1,)), ((), ())), preferred_element_type=F32)


def _dot_tn(a, b):
    return lax.dot_general(a, b, (((0,), (0,)), ((), ())), preferred_element_type=F32)


def _group_sum64(x, ones_bd):
    hi = x.astype(BF16)
    lo = (x - hi.astype(F32)).astype(BF16)
    return _dot(hi, ones_bd) + _dot(lo, ones_bd)


def _ones_blockdiag(n):
    r = lax.broadcasted_iota(jnp.int32, (n, n), 0) // DN_HD
    c = lax.broadcasted_iota(jnp.int32, (n, n), 1) // DN_HD
    return jnp.where(r == c, 1.0, 0.0).astype(BF16)


def _inproj_body(x_ref, w_ref, o_ref):
    o_ref[...] = _dot(x_ref[...].astype(BF16), w_ref[...])


def _inproj(x2, w):
    t, d = x2.shape
    n = w.shape[1]
    tm = min(512, t)
    return pl.pallas_call(
        _inproj_body,
        grid=(t // tm,),
        in_specs=[pl.BlockSpec((tm, d), lambda i: (i, 0)),
                  pl.BlockSpec((d, n), lambda i: (0, 0))],
        out_specs=pl.BlockSpec((tm, n), lambda i: (i, 0)),
        out_shape=jax.ShapeDtypeStruct((t, n), F32),
        name="inproj",
        compiler_params=_cparams(("parallel",)),
    )(x2, w)


CONV_PAD = 16
CONV_ROWS = 128


def _conv_body(p_ref, dw_ref, dwb_ref, g_ref, b_ref, pw_ref, o_ref, pad_scr, *, seq):
    zeros = jnp.zeros((CONV_PAD, CONV_W), F32)
    pad_scr[0:CONV_PAD, :] = zeros
    pad_scr[CONV_PAD + seq:2 * CONV_PAD + seq, :] = zeros

    def glu(i, carry):
        base = pl.multiple_of(i * CONV_ROWS, CONV_ROWS)
        p = p_ref[pl.ds(base, CONV_ROWS), :]
        pad_scr[pl.ds(base + CONV_PAD, CONV_ROWS), :] = p[:, :CONV_W] * _sigmoid(p[:, CONV_W:])
        return carry

    lax.fori_loop(0, seq // CONV_ROWS, glu, 0)
    half = (CONV_WIDTH - 1) // 2

    def conv(i, carry):
        base = pl.multiple_of(i * CONV_ROWS, CONV_ROWS)
        acc = jnp.zeros((CONV_ROWS, CONV_W), F32) + dwb_ref[...]
        win = pad_scr[pl.ds(base, CONV_ROWS + 2 * CONV_PAD), :]
        n_win = CONV_ROWS + 2 * CONV_PAD
        for r in range(SUBLANES):
            wr = win if r == 0 else pltpu.roll(win, n_win - r, 0)
            for j in range(CONV_WIDTH):
                off = CONV_PAD - half + j
                if off % SUBLANES == r:
                    acc = acc + wr[off - r:off - r + CONV_ROWS, :] * dw_ref[j:j + 1, :]
        y = _silu(_layer_norm(acc, g_ref[...], b_ref[...]))
        o_ref[pl.ds(base, CONV_ROWS), :] = _dot(y.astype(BF16), pw_ref[...])
        return carry

    lax.fori_loop(0, seq // CONV_ROWS, conv, 0)


def _conv_module(proj3, dw, dwb, g, b, pw):
    bsz, seq, _ = proj3.shape
    full = lambda shape: pl.BlockSpec(shape, lambda i: (0,) * len(shape))
    return pl.pallas_call(
        functools.partial(_conv_body, seq=seq),
        grid=(bsz,),
        in_specs=[pl.BlockSpec((None, seq, 2 * CONV_W), lambda i: (i, 0, COL_CONV // (2 * CONV_W))),
                  full((CONV_WIDTH, CONV_W)), full((1, CONV_W)), full((1, CONV_W)), full((1, CONV_W)),
                  full((CONV_W, CONV_W))],
        out_specs=pl.BlockSpec((None, seq, CONV_W), lambda i: (i, 0, 0)),
        out_shape=jax.ShapeDtypeStruct((bsz, seq, CONV_W), F32),
        scratch_shapes=[pltpu.VMEM((seq + 2 * CONV_PAD, CONV_W), F32)],
        name="conv_module",
        compiler_params=_cparams(("parallel",)),
    )(proj3, dw, dwb, g, b, pw)


ATTN_TQ = 2048
ATTN_SUB = 256
ATTN_KB = 512
ATTN_ONES = 16
ATTN_FOLD = 64


def _rope(x, cos, sin_signed):
    lane = lax.broadcasted_iota(jnp.int32, x.shape, 1)
    first = (lane % DIFF_HD) < (DIFF_HD // 2)
    rot = jnp.where(first, pltpu.roll(x, LANES - DIFF_HD // 2, 1), pltpu.roll(x, DIFF_HD // 2, 1))
    return x * cos + rot * sin_signed


def _attn_body(q_ref, k_ref, v_ref, cq_ref, sq_ref, ck_ref, sk_ref, dl_ref, g_ref, o_ref, kr_scr, vt_scr,
               *, lambda_init):
    @pl.when(pl.program_id(2) == 0)
    def _():
        kr_scr[...] = _rope(k_ref[...], ck_ref[...], sk_ref[...]).astype(BF16)
        vt_scr[0:DIFF_VD, :] = v_ref[...].T.astype(BF16)
        vt_scr[DIFF_VD:, :] = jnp.ones((ATTN_ONES, vt_scr.shape[1]), BF16)

    dl = dl_ref[...]
    lam = (jnp.exp(jnp.sum(dl[0:1] * dl[1:2], axis=-1, keepdims=True))
           - jnp.exp(jnp.sum(dl[2:3] * dl[3:4], axis=-1, keepdims=True)) + lambda_init)

    q = _rope(q_ref[...], cq_ref[...], sq_ref[...]) * (DIFF_HD ** -0.5 * math.log2(math.e))
    lane = lax.broadcasted_iota(jnp.int32, q.shape, 1)
    first_map = lane < DIFF_HD
    kr = kr_scr[...]
    vt = vt_scr[...]

    def fold_keys(x, op):
        part = op(x.reshape(x.shape[0] // ATTN_FOLD, ATTN_FOLD, x.shape[1]), axis=0)
        return op(part, axis=0, keepdims=True)

    n_sub = q.shape[0] // ATTN_SUB
    qms = [jnp.where(first_map if mp == 0 else jnp.logical_not(first_map), q, 0.0)[s * ATTN_SUB:(s + 1) * ATTN_SUB]
           for s in range(n_sub) for mp in range(2)]
    qbs = [qm.astype(BF16) for qm in qms]
    kb = min(ATTN_KB, kr.shape[0])
    m_run = acc = None
    for j in range(kr.shape[0] // kb):
        krj = kr[j * kb:(j + 1) * kb, :]
        vtj = vt[:, j * kb:(j + 1) * kb]
        sts = [_dot_nt(krj, qb) for qb in qbs]
        ms_ = [fold_keys(st, jnp.max) for st in sts]
        if j > 0:
            ms_ = [jnp.maximum(m, mo) for m, mo in zip(ms_, m_run)]
        ps = [jnp.exp2(st - m).astype(BF16) for st, m in zip(sts, ms_)]
        pv = [_dot(vtj, p) for p in ps]
        if j > 0:
            pv = [o + a * jnp.exp2(mo - m) for o, a, mo, m in zip(pv, acc, m_run, ms_)]
        m_run, acc = ms_, pv
    os_ = [o[0:DIFF_VD] / o[DIFF_VD:DIFF_VD + 1] for o in acc]
    for s in range(n_sub):
        ot = os_[2 * s] - lam * os_[2 * s + 1]
        ms = jnp.mean(ot * ot, axis=0, keepdims=True)
        o_ref[s * ATTN_SUB:(s + 1) * ATTN_SUB, :] = (ot * lax.rsqrt(ms + LN_EPS) * g_ref[...] * (1.0 - lambda_init)).T


def _diff_attention(proj3, cos_t, sin_t, diff_lambda, subln_g, lambda_init):
    bsz, seq, _ = proj3.shape
    tq = min(ATTN_TQ, seq)
    cq, ck, cv = COL_Q // LANES, COL_K // LANES, COL_V // LANES
    return pl.pallas_call(
        functools.partial(_attn_body, lambda_init=lambda_init),
        grid=(bsz, DIFF_HEADS, seq // tq),
        in_specs=[pl.BlockSpec((None, tq, LANES), lambda b, h, i: (b, i, cq + h)),
                  pl.BlockSpec((None, seq, LANES), lambda b, h, i: (b, 0, ck + h)),
                  pl.BlockSpec((None, seq, LANES), lambda b, h, i: (b, 0, cv + h)),
                  pl.BlockSpec((tq, LANES), lambda b, h, i: (i, 0)),
                  pl.BlockSpec((tq, LANES), lambda b, h, i: (i, 0)),
                  pl.BlockSpec((seq, LANES), lambda b, h, i: (0, 0)),
                  pl.BlockSpec((seq, LANES), lambda b, h, i: (0, 0)),
                  pl.BlockSpec((4, DIFF_HD), lambda b, h, i: (0, 0)),
                  pl.BlockSpec((DIFF_VD, 1), lambda b, h, i: (0, 0))],
        out_specs=pl.BlockSpec((None, tq, LANES), lambda b, h, i: (b, i, h)),
        out_shape=jax.ShapeDtypeStruct((bsz, seq, DIFF_HEADS * DIFF_VD), F32),
        scratch_shapes=[pltpu.VMEM((seq, LANES), BF16), pltpu.VMEM((DIFF_VD + ATTN_ONES, seq), BF16)],
        name="diff_attention",
        compiler_params=_cparams(("parallel", "parallel", "arbitrary")),
    )(proj3, proj3, proj3, cos_t, sin_t, cos_t, sin_t, diff_lambda, subln_g)


DN_PAD = 8
DN_ROWS = 256
DN_BLK = 16
PAIR = 2 * DN_CHUNK
DN_UNROLL = 4


def _dn_body(q_ref, k_ref, v_ref, z_ref, gt_ref, cwq_ref, cwk_ref, cwv_ref, prm_ref, ng_ref, o_ref,
             pad_scr, q_scr, k_scr, v_scr, g_scr, dec_scr, m_scr, n_scr, qe_scr, oi_scr, *prep_scr, seq):
    c = DN_CHUNK
    n_chunks = seq // c
    set_a, set_b = prep_scr[:len(prep_scr) // 2], prep_scr[len(prep_scr) // 2:]
    ones_bd = _ones_blockdiag(LANES)
    zeros = jnp.zeros((DN_PAD, LANES), F32)
    pad_scr[0:DN_PAD, :] = zeros
    pad_scr[DN_PAD + seq:2 * DN_PAD + seq, :] = zeros
    half = (DN_CONV - 1) // 2

    def conv_silu(src_ref, cw_ref, dst_scr, normalise):
        def copy(i, carry):
            base = pl.multiple_of(i * DN_ROWS, DN_ROWS)
            pad_scr[pl.ds(base + DN_PAD, DN_ROWS), :] = src_ref[pl.ds(base, DN_ROWS), :]
            return carry

        lax.fori_loop(0, seq // DN_ROWS, copy, 0)

        def conv(i, carry):
            base = pl.multiple_of(i * DN_ROWS, DN_ROWS)
            win = pad_scr[pl.ds(base, DN_ROWS + 2 * DN_PAD), :]
            acc = win[DN_PAD - half:DN_PAD - half + DN_ROWS, :] * cw_ref[0:1, :]
            for j in range(1, DN_CONV):
                off = DN_PAD - half + j
                acc = acc + win[off:off + DN_ROWS, :] * cw_ref[j:j + 1, :]
            y = _silu(acc)
            if normalise:
                y = y * lax.rsqrt(_group_sum64(y * y, ones_bd) + 1e-6)
            dst_scr[pl.ds(base, DN_ROWS), :] = y
            return carry

        lax.fori_loop(0, seq // DN_ROWS, conv, 0)

    conv_silu(q_ref, cwq_ref, q_scr, True)
    conv_silu(k_ref, cwk_ref, k_scr, True)
    conv_silu(v_ref, cwv_ref, v_scr, False)

    a_log = prm_ref[0:1, :]
    dt_bias = prm_ref[1:2, :]

    def gates(i, carry):
        base = pl.multiple_of(i * DN_ROWS, DN_ROWS)
        blk = gt_ref[pl.ds(base, DN_ROWS), :]
        lane = lax.broadcasted_iota(jnp.int32, blk.shape, 1)
        g_scr[pl.ds(base, DN_ROWS), :] = jnp.where(lane < 4, _sigmoid(blk),
                                                   -jnp.exp(a_log) * _softplus(blk + dt_bias))
        return carry

    lax.fori_loop(0, seq // DN_ROWS, gates, 0)

    row2 = lax.broadcasted_iota(jnp.int32, (PAIR, PAIR), 0)
    col2 = lax.broadcasted_iota(jnp.int32, (PAIR, PAIR), 1)
    same_head = (row2 // c) == (col2 // c)
    same_blk = (row2 // DN_BLK) == (col2 // DN_BLK)
    lane_cl = lax.broadcasted_iota(jnp.int32, (c, LANES), 1)
    row_cl = lax.broadcasted_iota(jnp.int32, (c, LANES), 0)
    head0 = lane_cl < DN_HD

    def stack(x):
        return jnp.concatenate([jnp.where(head0, x, 0.0), jnp.where(head0, 0.0, x)], axis=0)

    def fold(x):
        return x[:c] + x[c:]

    def col_pair(x, lane0):
        return jnp.concatenate([x[:, lane0:lane0 + 1], x[:, lane0 + 1:lane0 + 2]], axis=0)

    bd_state = (lax.broadcasted_iota(jnp.int32, (LANES, LANES), 0) // DN_HD) == \
               (lax.broadcasted_iota(jnp.int32, (LANES, LANES), 1) // DN_HD)

    def phase1_chunk(n):
        rows = pl.ds(pl.multiple_of(n * c, c), c)
        gc = g_scr[rows, :]
        fwd_cum, bwd_cum = gc, gc
        sh = 1
        while sh < c:
            fwd_cum = fwd_cum + jnp.where(row_cl >= sh, pltpu.roll(fwd_cum, sh, 0), 0.0)
            bwd_cum = bwd_cum + jnp.where(row_cl < c - sh, pltpu.roll(bwd_cum, c - sh, 0), 0.0)
            sh *= 2
        cum = jnp.where(lane_cl < 6, fwd_cum, bwd_cum)
        cum_t = cum.T
        tot = jnp.where(lane_cl[0:1] < 6, cum[c - 1:c, :], cum[0:1, :])

        kc = k_scr[rows, :]
        qc = q_scr[rows, :]
        vc = v_scr[rows, :]
        k2 = stack(kc)
        q2 = stack(qc)
        v2 = stack(vc)
        k2b = k2.astype(BF16)
        kq = _dot_nt(jnp.concatenate([k2b, q2.astype(BF16)], axis=0), k2b)
        kk = kq[:PAIR]
        qk = kq[PAIR:]
        return dict(n=n, rows=rows, gc=gc, cum=cum, cum_t=cum_t, tot=tot, k2=k2, q2=q2, v2=v2, kk=kk, qk=qk)

    def phase1_chain(ch, d):
        gc, cum, cum_t, tot = ch["gc"], ch["cum"], ch["cum_t"], ch["tot"]
        beta2 = col_pair(gc, 2 * d)
        cum2 = col_pair(cum, 4 + 2 * d)
        cum_row = jnp.concatenate([cum_t[4 + 2 * d:5 + 2 * d, :], cum_t[5 + 2 * d:6 + 2 * d, :]], axis=1)
        tot2 = jnp.concatenate([jnp.broadcast_to(tot[:, 4 + 2 * d:5 + 2 * d], (c, 1)),
                                jnp.broadcast_to(tot[:, 5 + 2 * d:6 + 2 * d], (c, 1))], axis=0)
        if d == 0:
            incl, strict = same_head & (row2 >= col2), same_head & (row2 > col2)
        else:
            incl, strict = same_head & (row2 <= col2), same_head & (row2 < col2)
        decay = jnp.exp(jnp.where(incl, cum2 - cum_row, -jnp.inf))
        lmat = jnp.where(strict, beta2 * ch["kk"] * decay, 0.0)
        rhs = jnp.concatenate([ch["v2"] * beta2, ch["k2"] * (beta2 * jnp.exp(cum2))], axis=1)
        qk_d = jnp.where(incl, ch["qk"] * decay, 0.0) * (DN_HD ** -0.5)
        qd2 = ch["q2"] * ((DN_HD ** -0.5) * jnp.exp(cum2))
        kd = fold(ch["k2"] * jnp.exp(tot2 - cum2))
        dec = jnp.where(lane_cl[0:1] < DN_HD, jnp.exp(tot[:, 4 + 2 * d:5 + 2 * d]),
                        jnp.exp(tot[:, 5 + 2 * d:6 + 2 * d]))
        return dict(d=d, n=ch["n"], rows=ch["rows"], lmat=lmat, rhs=rhs, qk_d=qk_d, qd2=qd2, kd=kd, dec=dec)

    unroll = min(DN_UNROLL, n_chunks // 2)
    n_chain = 2 * unroll

    def prepare(i, px, pz, pqk, pqd, pkd):
        chunks = [phase1_chunk(i * unroll + j) for j in range(unroll)]
        for idx, t in enumerate(phase1_chain(ch, d) for ch in chunks for d in range(2)):
            px[idx] = jnp.where(same_blk, -t["lmat"], 0.0).astype(BF16)
            pz[idx] = jnp.concatenate([jnp.where(same_blk, 0.0, t["lmat"]), t["rhs"]], axis=1)
            pqk[idx] = t["qk_d"].astype(BF16)
            pqd[idx] = fold(t["qd2"])
            pkd[idx] = t["kd"].astype(BF16)
            dec_scr[t["d"], t["n"]] = jnp.broadcast_to(t["dec"], (SUBLANES, LANES))

    def solve(i, px, pz, pqk, pqd, pkd):
        xs = [px[idx] for idx in range(n_chain)]
        zs = [pz[idx] for idx in range(n_chain)]
        bdot = lambda a, b: _dot(a.astype(BF16), b.astype(BF16))

        def apply_powers(ms, vs, n_squarings):
            for _ in range(n_squarings):
                both = [bdot(m, jnp.concatenate([m, v], axis=1)) for m, v in zip(ms, vs)]
                vs = [v + b[:, PAIR:] for v, b in zip(vs, both)]
                ms = [b[:, :PAIR] for b in both]
            return [v + bdot(m, v) for m, v in zip(ms, vs)]

        zs = apply_powers(xs, zs, int(math.log2(DN_BLK)) - 1)
        ys = [-z[:, :PAIR] for z in zs]
        rs = apply_powers(ys, [z[:, PAIR:] for z in zs], int(math.log2(DN_CHUNK // DN_BLK)) - 1)
        a_s = [_dot(pqk[idx], sol.astype(BF16)) for idx, sol in enumerate(rs)]
        mns = [_dot_tn(pkd[idx], fold(sol).astype(BF16)) for idx, sol in enumerate(rs)]
        for idx, (a, mn) in enumerate(zip(a_s, mns)):
            d, n = idx % 2, i * unroll + idx // 2
            rows = pl.ds(pl.multiple_of(n * c, c), c)
            n_scr[d, n] = jnp.where(bd_state, mn[:, :LANES], 0.0)
            m_scr[d, n] = jnp.where(bd_state, -mn[:, LANES:], 0.0).astype(BF16)
            qe_scr[d, rows, :] = (pqd[idx] - fold(a[:, LANES:])).astype(BF16)
            oi_scr[d, rows, :] = fold(a[:, :LANES])

    n_steps = n_chunks // unroll
    prepare(0, *set_a)

    def phase1(k, carry):
        solve(2 * k, *set_a)
        prepare(2 * k + 1, *set_b)
        solve(2 * k + 1, *set_b)
        prepare(jnp.minimum(2 * k + 2, n_steps - 1), *set_a)
        return carry

    lax.fori_loop(0, n_steps // 2, phase1, 0)

    o_ref[...] = jnp.zeros((seq, LANES), F32)

    def phase2(i, states):
        new_states = []
        for d in range(2):
            n = i if d == 0 else n_chunks - 1 - i
            rows = pl.ds(pl.multiple_of(n * c, c), c)
            state = states[d]
            sb = state.astype(BF16)
            o_ref[rows, :] = o_ref[rows, :] + oi_scr[d, rows, :] + _dot(qe_scr[d, rows, :], sb)
            new_states.append(state * dec_scr[d, n][0:1, :] + _dot(m_scr[d, n], sb) + n_scr[d, n])
        return tuple(new_states)

    zero_state = jnp.zeros((LANES, LANES), F32)
    lax.fori_loop(0, n_chunks, phase2, (zero_state, zero_state))

    def finish(i, carry):
        base = pl.multiple_of(i * DN_ROWS, DN_ROWS)
        o = o_ref[pl.ds(base, DN_ROWS), :]
        ms = _group_sum64(o * o, ones_bd) * (1.0 / DN_HD)
        o_ref[pl.ds(base, DN_ROWS), :] = o * lax.rsqrt(ms + LN_EPS) * ng_ref[...] * _silu(z_ref[pl.ds(base, DN_ROWS), :])
        return carry

    lax.fori_loop(0, seq // DN_ROWS, finish, 0)


def _deltanet(proj3, dn_conv, prm, norm_g2):
    bsz, seq, _ = proj3.shape
    cq, ck, cv, cz, cg = (COL_DNQ // LANES, COL_DNK // LANES, COL_DNV // LANES, COL_DNZ // LANES, COL_GATE // LANES)
    col = lambda c0: pl.BlockSpec((None, seq, LANES), lambda b, hp: (b, 0, c0 + hp))
    cw = lambda c0: pl.BlockSpec((DN_CONV, LANES), lambda b, hp: (0, c0 + hp))
    n_chunks = seq // DN_CHUNK
    n_chain = 2 * min(DN_UNROLL, n_chunks // 2)
    prep_set = [pltpu.VMEM((n_chain, PAIR, PAIR), BF16), pltpu.VMEM((n_chain, PAIR, PAIR + 2 * LANES), F32),
                pltpu.VMEM((n_chain, PAIR, PAIR), BF16), pltpu.VMEM((n_chain, DN_CHUNK, LANES), F32),
                pltpu.VMEM((n_chain, DN_CHUNK, LANES), BF16)]
    return pl.pallas_call(
        functools.partial(_dn_body, seq=seq),
        grid=(bsz, DN_HEADS // 2),
        in_specs=[col(cq), col(ck), col(cv), col(cz), col(cg), cw(0), cw(2), cw(4),
                  pl.BlockSpec((None, SUBLANES, LANES), lambda b, hp: (hp, 0, 0)),
                  pl.BlockSpec((1, LANES), lambda b, hp: (0, 0))],
        out_specs=pl.BlockSpec((None, seq, LANES), lambda b, hp: (b, 0, hp)),
        out_shape=jax.ShapeDtypeStruct((bsz, seq, DN_HEADS * DN_HD), F32),
        scratch_shapes=[pltpu.VMEM((seq + 2 * DN_PAD, LANES), F32),
                        pltpu.VMEM((seq, LANES), F32), pltpu.VMEM((seq, LANES), F32), pltpu.VMEM((seq, LANES), F32),
                        pltpu.VMEM((seq, LANES), F32),
                        pltpu.VMEM((2, n_chunks, SUBLANES, LANES), F32),
                        pltpu.VMEM((2, n_chunks, LANES, LANES), BF16),
                        pltpu.VMEM((2, n_chunks, LANES, LANES), F32),
                        pltpu.VMEM((2, seq, LANES), BF16),
                        pltpu.VMEM((2, seq, LANES), F32)] + prep_set + prep_set,
        name="deltanet",
        compiler_params=_cparams(("parallel", "parallel")),
    )(proj3, proj3, proj3, proj3, proj3, dn_conv, dn_conv, dn_conv, prm, norm_g2)


def _outproj_body(yc_ref, yd_ref, yn_ref, x_ref, w_ref, g_ref, b_ref, o_ref):
    mix = jnp.concatenate([yc_ref[...], yd_ref[...], yn_ref[...]], axis=-1).astype(BF16)
    h = _dot(mix, w_ref[...])
    o_ref[...] = _layer_norm(DEEPNORM_ALPHA * x_ref[...] + h, g_ref[...], b_ref[...])


def _outproj_ln(yc, yd, yn, x2, w, g, b):
    t, d = x2.shape
    tm = min(512, t)
    row = lambda n: pl.BlockSpec((tm, n), lambda i: (i, 0))
    full = lambda shape: pl.BlockSpec(shape, lambda i: (0,) * len(shape))
    return pl.pallas_call(
        _outproj_body,
        grid=(t // tm,),
        in_specs=[row(yc.shape[1]), row(yd.shape[1]), row(yn.shape[1]), row(d), full(w.shape), full((1, d)), full((1, d))],
        out_specs=row(d),
        out_shape=jax.ShapeDtypeStruct((t, d), F32),
        name="outproj_ln",
        compiler_params=_cparams(("parallel",)),
    )(yc, yd, yn, x2, w, g, b)


def _ffn_body(x_ref, w1_ref, w3_ref, w2_ref, g_ref, b_ref, o_ref):
    x = x_ref[...]
    xb = x.astype(BF16)
    h = _silu(_dot(xb, w1_ref[...])) * _dot(xb, w3_ref[...])
    f = _dot(h.astype(BF16), w2_ref[...])
    o_ref[...] = _layer_norm(DEEPNORM_ALPHA * x + f, g_ref[...], b_ref[...])


def _ffn_ln(x2, w1, w3, w2, g, b):
    t, d = x2.shape
    f = w1.shape[1]
    tm = min(512, t)
    row = pl.BlockSpec((tm, d), lambda i: (i, 0))
    once = lambda shape: pl.BlockSpec(shape, lambda i: (0,) * len(shape), pipeline_mode=pl.Buffered(1))
    return pl.pallas_call(
        _ffn_body,
        grid=(t // tm,),
        in_specs=[row, once((d, f)), once((d, f)), once((f, d)), once((1, d)), once((1, d))],
        out_specs=row,
        out_shape=jax.ShapeDtypeStruct((t, d), F32),
        name="ffn_ln",
        compiler_params=_cparams(("parallel",)),
    )(x2, w1, w3, w2, g, b)


MOE_TB = 512
MOE_RUN = SUBLANES
MOE_BUF = 2 * MOE_TB + N_EXPERTS * MOE_RUN
META_P1, META_P2, META_G1, META_G2 = range(4)


def _moe_slots(t):
    n_blocks = t // min(MOE_TB, t)
    raw = 2 * t + n_blocks * N_EXPERTS * (MOE_RUN - 1) + N_EXPERTS * (MOE_TM - 1)
    return ((raw + MOE_TM - 1) // MOE_TM) * MOE_TM


def _router_body(x_ref, rw_ref, meta_ref, cnt_ref):
    tm = x_ref.shape[0]
    xh, xl = _split_bf16(x_ref[...])
    wh, wl = _split_bf16(rw_ref[...])
    logits = _dot(xh, wh) + _dot(xl, wh) + _dot(xh, wl)
    lane = lax.broadcasted_iota(jnp.int32, logits.shape, 1)
    logits = jnp.where(lane < N_EXPERTS, logits, -jnp.inf)
    m1 = jnp.max(logits, axis=-1, keepdims=True)
    e1 = jnp.min(jnp.where(logits == m1, lane, LANES), axis=-1, keepdims=True)
    rest = jnp.where(lane == e1, -jnp.inf, logits)
    m2 = jnp.max(rest, axis=-1, keepdims=True)
    e2 = jnp.min(jnp.where(rest == m2, lane, LANES), axis=-1, keepdims=True)
    t = jnp.exp(m2 - m1)
    g1 = 1.0 / (1.0 + t)
    g2 = t / (1.0 + t)

    sel = jnp.where((lane == e1) | (lane == e2), 1.0, 0.0)
    r = lax.broadcasted_iota(jnp.int32, (tm, tm), 0)
    c = lax.broadcasted_iota(jnp.int32, (tm, tm), 1)
    strict_lower = jnp.where(r > c, 1.0, 0.0).astype(BF16)
    rank = _dot(strict_lower, sel.astype(BF16))
    cnt = jnp.sum(sel, axis=0, keepdims=True)
    run = jnp.floor((cnt + (MOE_RUN - 1.0)) * (1.0 / MOE_RUN)) * MOE_RUN
    lane1 = lane[0:1]
    start = run
    sh = 1
    while sh < N_EXPERTS:
        start = start + jnp.where(lane1 >= sh, pltpu.roll(start, sh, 1), 0.0)
        sh *= 2
    pos = rank + (start - run)
    p1 = jnp.sum(jnp.where(lane == e1, pos, 0.0), axis=-1, keepdims=True)
    p2 = jnp.sum(jnp.where(lane == e2, pos, 0.0), axis=-1, keepdims=True)
    cnt_ref[...] = jnp.broadcast_to(cnt, cnt_ref.shape)

    meta = jnp.where(lane == META_P1, p1, 0.0)
    meta = jnp.where(lane == META_P2, p2, meta)
    meta = jnp.where(lane == META_G1, g1, meta)
    meta = jnp.where(lane == META_G2, g2, meta)
    meta_ref[...] = meta


def _router(x2, rw_pad):
    t, d = x2.shape
    tm = min(MOE_TB, t)
    return pl.pallas_call(
        _router_body,
        grid=(t // tm,),
        in_specs=[pl.BlockSpec((tm, d), lambda i: (i, 0)), pl.BlockSpec((d, LANES), lambda i: (0, 0))],
        out_specs=[pl.BlockSpec((tm, LANES), lambda i: (i, 0)),
                   pl.BlockSpec((None, SUBLANES, LANES), lambda i: (i, 0, 0))],
        out_shape=[jax.ShapeDtypeStruct((t, LANES), F32),
                   jax.ShapeDtypeStruct((t // tm, SUBLANES, LANES), F32)],
        name="router",
        compiler_params=_cparams(("parallel",)),
    )(x2, rw_pad)


def _for_each_run_piece(tab_ref, n_runs, blk, fn):
    for e in range(N_EXPERTS):
        k = blk * N_EXPERTS + e
        lo, run, dst = tab_ref[k], tab_ref[n_runs + k], tab_ref[2 * n_runs + k]
        bit = MOE_TB
        while bit >= MOE_RUN:
            done = run & ~(2 * bit - 1)

            @pl.when((run & bit) != 0)
            def _():
                fn(pl.multiple_of(lo + done, MOE_RUN), pl.multiple_of(dst + done, MOE_RUN), bit)

            bit //= 2


def _dispatch_body(tab_ref, x_ref, meta_ref, xs_ref, buf, sem, *, n_runs):
    blk = pl.program_id(0)
    last = pl.num_programs(0) - 1
    cur = blk % 2
    tb = x_ref.shape[0]

    def copy(s, buf_row, sorted_row, n):
        return pltpu.make_async_copy(buf.at[s, pl.ds(buf_row, n), :], xs_ref.at[pl.ds(sorted_row, n), :], sem.at[s])

    @pl.when(blk == 0)
    def _():
        buf[0] = jnp.zeros((MOE_BUF, buf.shape[2]), F32)

        def tail(e):
            row = pl.multiple_of(tab_ref[3 * n_runs + e], MOE_TM)
            return pltpu.make_async_copy(buf.at[0, pl.ds(0, MOE_TM), :], xs_ref.at[pl.ds(row, MOE_TM), :], sem.at[0])

        for e in range(N_EXPERTS):
            pl.when(tab_ref[3 * n_runs + e] >= 0)(lambda e=e: tail(e).start())
        for e in range(N_EXPERTS):
            pl.when(tab_ref[3 * n_runs + e] >= 0)(lambda e=e: tail(e).wait())

        def unused_tile(j, carry):
            row = pl.multiple_of(j * MOE_TM, MOE_TM)
            c = pltpu.make_async_copy(buf.at[0, pl.ds(0, MOE_TM), :], xs_ref.at[pl.ds(row, MOE_TM), :], sem.at[0])
            c.start()
            c.wait()
            return carry

        lax.fori_loop(tab_ref[3 * n_runs + N_EXPERTS], xs_ref.shape[0] // MOE_TM, unused_tile, 0)

    meta_t = meta_ref[...].T
    p1 = meta_t[META_P1:META_P1 + 1, :]
    p2 = meta_t[META_P2:META_P2 + 1, :]
    slot = lax.broadcasted_iota(jnp.int32, (MOE_BUF, tb), 0).astype(F32)
    onehot = jnp.where((slot == p1) | (slot == p2), 1.0, 0.0).astype(BF16)
    buf[cur] = _dot(onehot, x_ref[...].astype(BF16))

    _for_each_run_piece(tab_ref, n_runs, blk, lambda *a: copy(cur, *a).start())

    @pl.when(blk > 0)
    def _():
        _for_each_run_piece(tab_ref, n_runs, blk - 1, lambda *a: copy(1 - cur, *a).wait())

    @pl.when(blk == last)
    def _():
        _for_each_run_piece(tab_ref, n_runs, blk, lambda *a: copy(cur, *a).wait())


def _dispatch(tab, x2, meta, n_slots):
    t, d = x2.shape
    tb = min(MOE_TB, t)
    grid_spec = pltpu.PrefetchScalarGridSpec(
        num_scalar_prefetch=1,
        grid=(t // tb,),
        in_specs=[pl.BlockSpec((tb, d), lambda i, tab: (i, 0)),
                  pl.BlockSpec((tb, LANES), lambda i, tab: (i, 0))],
        out_specs=pl.BlockSpec(memory_space=pl.ANY),
        scratch_shapes=[pltpu.VMEM((2, MOE_BUF, d), F32), pltpu.SemaphoreType.DMA((2,))],
    )
    return pl.pallas_call(
        functools.partial(_dispatch_body, n_runs=(t // tb) * N_EXPERTS),
        grid_spec=grid_spec,
        out_shape=jax.ShapeDtypeStruct((n_slots, d), F32),
        name="moe_dispatch",
        compiler_params=pltpu.CompilerParams(dimension_semantics=("arbitrary",), vmem_limit_bytes=VMEM_LIMIT,
                                             has_side_effects=True),
    )(tab, x2, meta)


def _experts_body(te_ref, nu_ref, xs_ref, w1_ref, w3_ref, w2_ref, ys_ref, xb_scr, acc_scr):
    i = pl.program_id(0)
    f = pl.program_id(1)
    used = i < nu_ref[0]

    @pl.when(used & (f == 0))
    def _():
        xb_scr[...] = xs_ref[...].astype(BF16)

    @pl.when(used)
    def _():
        xb = xb_scr[...]
        h = _silu(_dot(xb, w1_ref[...])) * _dot(xb, w3_ref[...])
        part = _dot(h.astype(BF16), w2_ref[...])

        @pl.when(f == 0)
        def _():
            acc_scr[...] = part

        @pl.when(f > 0)
        def _():
            acc_scr[...] = acc_scr[...] + part

    @pl.when(f == pl.num_programs(1) - 1)
    def _():
        @pl.when(used)
        def _():
            ys_ref[...] = acc_scr[...]

        @pl.when(jnp.logical_not(used))
        def _():
            ys_ref[...] = jnp.zeros_like(ys_ref)


def _experts(tile_expert, n_used, xs, w1, w3, w2):
    n_slots, d = xs.shape
    n_tiles = n_slots // MOE_TM
    nf = w1.shape[2] // MOE_TF

    def fidx(i, f, te, nu):
        return jnp.where(i < nu[0], f, nf - 1)

    grid_spec = pltpu.PrefetchScalarGridSpec(
        num_scalar_prefetch=2,
        grid=(n_tiles, nf),
        in_specs=[pl.BlockSpec((MOE_TM, d), lambda i, f, te, nu: (jnp.minimum(i, nu[0] - 1), 0)),
                  pl.BlockSpec((None, d, MOE_TF), lambda i, f, te, nu: (te[i], 0, fidx(i, f, te, nu))),
                  pl.BlockSpec((None, d, MOE_TF), lambda i, f, te, nu: (te[i], 0, fidx(i, f, te, nu))),
                  pl.BlockSpec((None, MOE_TF, d), lambda i, f, te, nu: (te[i], fidx(i, f, te, nu), 0))],
        out_specs=pl.BlockSpec((MOE_TM, d), lambda i, f, te, nu: (i, 0)),
        scratch_shapes=[pltpu.VMEM((MOE_TM, d), BF16), pltpu.VMEM((MOE_TM, d), F32)],
    )
    return pl.pallas_call(
        _experts_body,
        grid_spec=grid_spec,
        out_shape=jax.ShapeDtypeStruct((n_slots, d), F32),
        name="moe_experts",
        compiler_params=_cparams(("arbitrary", "arbitrary")),
    )(tile_expert, n_used, xs, w1, w3, w2)


def _split_bf16(v):
    hi = v.astype(BF16)
    return hi, (v - hi.astype(F32)).astype(BF16)


def _combine_body(tab_ref, x_ref, meta_ref, ys_ref, g_ref, b_ref, o_ref, ybuf, sem, *, n_runs):
    blk = pl.program_id(0)
    last = pl.num_programs(0) - 1
    cur = blk % 2
    tb = x_ref.shape[0]

    def copy(s, buf_row, sorted_row, n):
        return pltpu.make_async_copy(ys_ref.at[pl.ds(sorted_row, n), :], ybuf.at[s, pl.ds(buf_row, n), :], sem.at[s])

    @pl.when(blk == 0)
    def _():
        ybuf[...] = jnp.zeros_like(ybuf)
        _for_each_run_piece(tab_ref, n_runs, blk, lambda *a: copy(cur, *a).start())

    @pl.when(blk < last)
    def _():
        _for_each_run_piece(tab_ref, n_runs, blk + 1, lambda *a: copy(1 - cur, *a).start())

    _for_each_run_piece(tab_ref, n_runs, blk, lambda *a: copy(cur, *a).wait())

    meta = meta_ref[...]
    p1 = meta[:, META_P1:META_P1 + 1]
    p2 = meta[:, META_P2:META_P2 + 1]
    g1 = meta[:, META_G1:META_G1 + 1]
    g2 = meta[:, META_G2:META_G2 + 1]
    slot = lax.broadcasted_iota(jnp.int32, (tb, MOE_BUF), 1).astype(F32)
    weights = jnp.where(slot == p1, g1, 0.0) + jnp.where(slot == p2, g2, 0.0)
    wh, wl = _split_bf16(weights)
    yh, yl = _split_bf16(ybuf[cur])
    f = _dot(wh, yh) + _dot(wh, yl) + _dot(wl, yh)
    o_ref[...] = _layer_norm(DEEPNORM_ALPHA * x_ref[...] + f, g_ref[...], b_ref[...])


def _combine_ln(tab, x2, meta, ys, g, b):
    t, d = x2.shape
    tb = min(MOE_TB, t)
    grid_spec = pltpu.PrefetchScalarGridSpec(
        num_scalar_prefetch=1,
        grid=(t // tb,),
        in_specs=[pl.BlockSpec((tb, d), lambda i, tab: (i, 0)),
                  pl.BlockSpec((tb, LANES), lambda i, tab: (i, 0)),
                  pl.BlockSpec(memory_space=pl.ANY),
                  pl.BlockSpec((1, d), lambda i, tab: (0, 0)),
                  pl.BlockSpec((1, d), lambda i, tab: (0, 0))],
        out_specs=pl.BlockSpec((tb, d), lambda i, tab: (i, 0)),
        scratch_shapes=[pltpu.VMEM((2, MOE_BUF, d), F32), pltpu.SemaphoreType.DMA((2,))],
    )
    return pl.pallas_call(
        functools.partial(_combine_body, n_runs=(t // tb) * N_EXPERTS),
        grid_spec=grid_spec,
        out_shape=jax.ShapeDtypeStruct((t, d), F32),
        name="moe_combine_ln",
        compiler_params=_cparams(("arbitrary",)),
    )(tab, x2, meta, ys, g, b)


def _moe_ln(x2, rw_pad, w1, w3, w2, g, b):
    t, d = x2.shape
    meta, blk_cnt = _router(x2, rw_pad)
    cnt = blk_cnt[:, 0, :N_EXPERTS].astype(jnp.int32)
    run = ((cnt + MOE_RUN - 1) // MOE_RUN) * MOE_RUN
    lo = jnp.cumsum(run, axis=1) - run
    before = jnp.cumsum(run, axis=0) - run
    padded = ((jnp.sum(run, axis=0) + MOE_TM - 1) // MOE_TM) * MOE_TM
    ends = jnp.cumsum(padded)
    starts = ends - padded
    tails = jnp.where(padded > 0, ends - MOE_TM, -1)
    n_slots = _moe_slots(t)
    n_tiles = n_slots // MOE_TM
    n_used = (ends[-1] // MOE_TM).astype(jnp.int32)
    tab = jnp.concatenate([lo.reshape(-1), run.reshape(-1), (starts[None, :] + before).reshape(-1),
                           tails, n_used.reshape(1)]).astype(jnp.int32)
    tile_start = jnp.arange(n_tiles, dtype=jnp.int32) * MOE_TM
    tile_expert = jnp.sum((tile_start[:, None] >= ends[None, :]).astype(jnp.int32), axis=1)
    last_expert = jnp.sum((((n_used - 1) * MOE_TM) >= ends).astype(jnp.int32))
    tile_expert = jnp.where(jnp.arange(n_tiles) < n_used, tile_expert, last_expert).astype(jnp.int32)

    xs = _dispatch(tab, x2, meta, n_slots)
    ys = _experts(tile_expert, n_used.reshape(1), xs, w1, w3, w2)
    return _combine_ln(tab, x2, meta, ys, g, b)


CAST_ROWS = 512


def _cast_body(w_ref, o_ref):
    o_ref[...] = w_ref[...].astype(BF16)


def _expert_weights_bf16(w, layer):
    _, n_exp, rows, cols = w.shape
    return pl.pallas_call(
        _cast_body,
        grid=(n_exp, rows // CAST_ROWS),
        in_specs=[pl.BlockSpec((None, None, CAST_ROWS, cols), lambda e, r: (layer, e, r, 0))],
        out_specs=pl.BlockSpec((None, CAST_ROWS, cols), lambda e, r: (e, r, 0)),
        out_shape=jax.ShapeDtypeStruct((n_exp, rows, cols), BF16),
        name="expert_weight_cast",
        compiler_params=_cparams(("parallel", "parallel")),
    )(w)
def _rope_tables(seq):
    inv = ROPE_THETA ** (-jnp.arange(0, DIFF_HD, 2, dtype=F32) / DIFF_HD)
    ang = jnp.arange(seq, dtype=F32)[:, None] * inv[None, :]
    cos, sin = jnp.cos(ang), jnp.sin(ang)
    cos_t = jnp.tile(cos, (1, LANES // cos.shape[1]))
    sin_t = jnp.tile(jnp.concatenate([-sin, sin], axis=1), (1, LANES // (2 * sin.shape[1])))
    return cos_t, sin_t


def _gate_columns():
    cols = []
    for hp in range(DN_HEADS // 2):
        blk = [PROJ_RAW - 16 + d * DN_HEADS + 2 * hp + hl for d in range(2) for hl in range(2)]
        blk += [PROJ_RAW - 8 + d * DN_HEADS + 2 * hp + hl for d in range(2) for hl in range(2)]
        cols.append(blk)
    return cols


def _prep_w_in(w_in_l):
    parts = [w_in_l[:, :COL_GATE]]
    for blk in _gate_columns():
        parts.append(w_in_l[:, jnp.array(blk)])
        parts.append(jnp.zeros((w_in_l.shape[0], LANES - len(blk)), w_in_l.dtype))
    return jnp.concatenate(parts, axis=1).astype(BF16)


def _prep_dn_params(a_log_l, dt_bias_l):
    out = []
    for hp in range(DN_HEADS // 2):
        idx = [(d, 2 * hp + hl) for d in range(2) for hl in range(2)]
        a = jnp.stack([a_log_l[d, h] for d, h in idx])
        t = jnp.stack([dt_bias_l[d, h] for d, h in idx])
        blk = jnp.zeros((SUBLANES, LANES), F32)
        blk = blk.at[0, 4:8].set(a).at[1, 4:8].set(t)
        out.append(blk)
    return jnp.stack(out)


def kernel(x, w_in, w_o, ln1_g, ln1_b, ln2_g, ln2_b, conv_dw, conv_dw_b, conv_ln_g, conv_ln_b, conv_pw,
           diff_lambda, diff_subln_g, dn_conv, dn_a_log, dn_dt_bias, dn_norm_g,
           ffn_w1, ffn_w3, ffn_w2, router_w, moe_w1, moe_w3, moe_w2):
    bsz, seq, d = x.shape
    t = bsz * seq
    cos_t, sin_t = _rope_tables(seq)
    x2 = x.reshape(t, d)
    row = lambda v: v.reshape(1, -1)
    for layer in range(DEPTH):
        lambda_init = 0.8 - 0.6 * math.exp(-0.3 * layer)
        proj = _inproj(x2, _prep_w_in(w_in[layer]))
        proj3 = proj.reshape(bsz, seq, PROJ_PAD)
        y_conv = _conv_module(proj3, conv_dw[layer], row(conv_dw_b[layer]), row(conv_ln_g[layer]),
                              row(conv_ln_b[layer]), conv_pw[layer].astype(BF16))
        y_diff = _diff_attention(proj3, cos_t, sin_t, diff_lambda[layer], diff_subln_g[layer].reshape(-1, 1),
                                 lambda_init)
        y_dn = _deltanet(proj3, dn_conv[layer], _prep_dn_params(dn_a_log[layer], dn_dt_bias[layer]),
                         row(jnp.tile(dn_norm_g[layer], 2)))
        x2 = _outproj_ln(y_conv.reshape(t, -1), y_diff.reshape(t, -1), y_dn.reshape(t, -1), x2,
                         w_o[layer].astype(BF16), row(ln1_g[layer]), row(ln1_b[layer]))
        j = layer // 2
        if layer % 2 == 0:
            x2 = _ffn_ln(x2, ffn_w1[j].astype(BF16), ffn_w3[j].astype(BF16), ffn_w2[j].astype(BF16),
                         row(ln2_g[layer]), row(ln2_b[layer]))
        else:
            rw_pad = jnp.pad(router_w[j], ((0, 0), (0, LANES - N_EXPERTS)))
            x2 = _moe_ln(x2, rw_pad, _expert_weights_bf16(moe_w1, j), _expert_weights_bf16(moe_w3, j),
                         _expert_weights_bf16(moe_w2, j), row(ln2_g[layer]), row(ln2_b[layer]))
    return x2.reshape(bsz, seq, d)
```

```python
import functools
import math

import jax
import jax.numpy as jnp
from jax import lax
from jax.experimental import pallas as pl
from jax.experimental.pallas import tpu as pltpu

F32 = jnp.float32
BF16 = jnp.bfloat16

D_MODEL = 1024
DEPTH = 4
CONV_W = D_MODEL // 4
CONV_WIDTH = 31
DIFF_HEADS = 4
DIFF_HD = D_MODEL // 16
DIFF_VD = 2 * DIFF_HD
DN_HEADS = 4
DN_HD = D_MODEL // 16
DN_CONV = 5
DN_CHUNK = 64
ROPE_THETA = 10000.0
D_FF = 11 * D_MODEL // 4
N_EXPERTS = 8
D_FF_EXPERT = 7 * D_MODEL // 2
DEEPNORM_ALPHA = (2 * DEPTH) ** 0.25
LN_EPS = 1e-5

LANES = 128
SUBLANES = 8
VMEM_LIMIT = 56 * 2 ** 20

COL_CONV = 0
COL_Q = 512
COL_K = 1024
COL_V = 1536
COL_DNQ = 2048
COL_DNK = 2304
COL_DNV = 2560
COL_DNZ = 2816
COL_GATE = 3072
PROJ_RAW = 3088
PROJ_PAD = COL_GATE + 2 * LANES

MOE_TM = 512
MOE_TF = 1792


def _cparams(sem):
    return pltpu.CompilerParams(dimension_semantics=sem, vmem_limit_bytes=VMEM_LIMIT)


def _sigmoid(x):
    return 1.0 / (1.0 + jnp.exp(-x))


def _silu(x):
    return x * _sigmoid(x)


def _softplus(x):
    return jnp.maximum(x, 0.0) + jnp.log1p(jnp.exp(-jnp.abs(x)))


def _layer_norm(x, g, b):
    mu = jnp.mean(x, axis=-1, keepdims=True)
    xc = x - mu
    var = jnp.mean(xc * xc, axis=-1, keepdims=True)
    return xc * lax.rsqrt(var + LN_EPS) * g + b


def _dot(a, b):
    return jnp.dot(a, b, preferred_element_type=F32)


def _dot_nt(a, b):
    return lax.dot_general(a, b, (((1,), (1,)), ((), ())), preferred_element_type=F32)


def _dot_tn(a, b):
    return lax.dot_general(a, b, (((0,), (0,)), ((), ())), preferred_element_type=F32)


def _group_sum64(x, ones_bd):
    hi = x.astype(BF16)
    lo = (x - hi.astype(F32)).astype(BF16)
    return _dot(hi, ones_bd) + _dot(lo, ones_bd)


def _ones_blockdiag(n):
    r = lax.broadcasted_iota(jnp.int32, (n, n), 0) // DN_HD
    c = lax.broadcasted_iota(jnp.int32, (n, n), 1) // DN_HD
    return jnp.where(r == c, 1.0, 0.0).astype(BF16)


def _inproj_body(x_ref, w_ref, o_ref):
    o_ref[...] = _dot(x_ref[...].astype(BF16), w_ref[...])


def _inproj(x2, w):
    t, d = x2.shape
    n = w.shape[1]
    tm = min(512, t)
    return pl.pallas_call(
        _inproj_body,
        grid=(t // tm,),
        in_specs=[pl.BlockSpec((tm, d), lambda i: (i, 0)),
                  pl.BlockSpec((d, n), lambda i: (0, 0))],
        out_specs=pl.BlockSpec((tm, n), lambda i: (i, 0)),
        out_shape=jax.ShapeDtypeStruct((t, n), F32),
        name="inproj",
        compiler_params=_cparams(("parallel",)),
    )(x2, w)


CONV_PAD = 16
CONV_ROWS = 128


def _conv_body(p_ref, dw_ref, dwb_ref, g_ref, b_ref, pw_ref, o_ref, pad_scr, *, seq):
    zeros = jnp.zeros((CONV_PAD, CONV_W), F32)
    pad_scr[0:CONV_PAD, :] = zeros
    pad_scr[CONV_PAD + seq:2 * CONV_PAD + seq, :] = zeros

    def glu(i, carry):
        base = pl.multiple_of(i * CONV_ROWS, CONV_ROWS)
        p = p_ref[pl.ds(base, CONV_ROWS), :]
        pad_scr[pl.ds(base + CONV_PAD, CONV_ROWS), :] = p[:, :CONV_W] * _sigmoid(p[:, CONV_W:])
        return carry

    lax.fori_loop(0, seq // CONV_ROWS, glu, 0)
    half = (CONV_WIDTH - 1) // 2

    def conv(i, carry):
        base = pl.multiple_of(i * CONV_ROWS, CONV_ROWS)
        acc = jnp.zeros((CONV_ROWS, CONV_W), F32) + dwb_ref[...]
        win = pad_scr[pl.ds(base, CONV_ROWS + 2 * CONV_PAD), :]
        n_win = CONV_ROWS + 2 * CONV_PAD
        for r in range(SUBLANES):
            wr = win if r == 0 else pltpu.roll(win, n_win - r, 0)
            for j in range(CONV_WIDTH):
                off = CONV_PAD - half + j
                if off % SUBLANES == r:
                    acc = acc + wr[off - r:off - r + CONV_ROWS, :] * dw_ref[j:j + 1, :]
        y = _silu(_layer_norm(acc, g_ref[...], b_ref[...]))
        o_ref[pl.ds(base, CONV_ROWS), :] = _dot(y.astype(BF16), pw_ref[...])
        return carry

    lax.fori_loop(0, seq // CONV_ROWS, conv, 0)


def _conv_module(proj3, dw, dwb, g, b, pw):
    bsz, seq, _ = proj3.shape
    full = lambda shape: pl.BlockSpec(shape, lambda i: (0,) * len(shape))
    return pl.pallas_call(
        functools.partial(_conv_body, seq=seq),
        grid=(bsz,),
        in_specs=[pl.BlockSpec((None, seq, 2 * CONV_W), lambda i: (i, 0, COL_CONV // (2 * CONV_W))),
                  full((CONV_WIDTH, CONV_W)), full((1, CONV_W)), full((1, CONV_W)), full((1, CONV_W)),
                  full((CONV_W, CONV_W))],
        out_specs=pl.BlockSpec((None, seq, CONV_W), lambda i: (i, 0, 0)),
        out_shape=jax.ShapeDtypeStruct((bsz, seq, CONV_W), F32),
        scratch_shapes=[pltpu.VMEM((seq + 2 * CONV_PAD, CONV_W), F32)],
        name="conv_module",
        compiler_params=_cparams(("parallel",)),
    )(proj3, dw, dwb, g, b, pw)


ATTN_TQ = 2048
ATTN_SUB = 256
ATTN_KB = 512
ATTN_ONES = 16
ATTN_FOLD = 64


def _rope(x, cos, sin_signed):
    lane = lax.broadcasted_iota(jnp.int32, x.shape, 1)
    first = (lane % DIFF_HD) < (DIFF_HD // 2)
    rot = jnp.where(first, pltpu.roll(x, LANES - DIFF_HD // 2, 1), pltpu.roll(x, DIFF_HD // 2, 1))
    return x * cos + rot * sin_signed


def _attn_body(q_ref, k_ref, v_ref, cq_ref, sq_ref, ck_ref, sk_ref, dl_ref, g_ref, o_ref, kr_scr, vt_scr,
               *, lambda_init):
    @pl.when(pl.program_id(2) == 0)
    def _():
        kr_scr[...] = _rope(k_ref[...], ck_ref[...], sk_ref[...]).astype(BF16)
        vt_scr[0:DIFF_VD, :] = v_ref[...].T.astype(BF16)
        vt_scr[DIFF_VD:, :] = jnp.ones((ATTN_ONES, vt_scr.shape[1]), BF16)

    dl = dl_ref[...]
    lam = (jnp.exp(jnp.sum(dl[0:1] * dl[1:2], axis=-1, keepdims=True))
           - jnp.exp(jnp.sum(dl[2:3] * dl[3:4], axis=-1, keepdims=True)) + lambda_init)

    q = _rope(q_ref[...], cq_ref[...], sq_ref[...]) * (DIFF_HD ** -0.5 * math.log2(math.e))
    lane = lax.broadcasted_iota(jnp.int32, q.shape, 1)
    first_map = lane < DIFF_HD
    kr = kr_scr[...]
    vt = vt_scr[...]

    def fold_keys(x, op):
        part = op(x.reshape(x.shape[0] // ATTN_FOLD, ATTN_FOLD, x.shape[1]), axis=0)
        return op(part, axis=0, keepdims=True)

    n_sub = q.shape[0] // ATTN_SUB
    qms = [jnp.where(first_map if mp == 0 else jnp.logical_not(first_map), q, 0.0)[s * ATTN_SUB:(s + 1) * ATTN_SUB]
           for s in range(n_sub) for mp in range(2)]
    qbs = [qm.astype(BF16) for qm in qms]
    kb = min(ATTN_KB, kr.shape[0])
    m_run = acc = None
    for j in range(kr.shape[0] // kb):
        krj = kr[j * kb:(j + 1) * kb, :]
        vtj = vt[:, j * kb:(j + 1) * kb]
        sts = [_dot_nt(krj, qb) for qb in qbs]
        ms_ = [fold_keys(st, jnp.max) for st in sts]
        if j > 0:
            ms_ = [jnp.maximum(m, mo) for m, mo in zip(ms_, m_run)]
        ps = [jnp.exp2(st - m).astype(BF16) for st, m in zip(sts, ms_)]
        pv = [_dot(vtj, p) for p in ps]
        if j > 0:
            pv = [o + a * jnp.exp2(mo - m) for o, a, mo, m in zip(pv, acc, m_run, ms_)]
        m_run, acc = ms_, pv
    os_ = [o[0:DIFF_VD] / o[DIFF_VD:DIFF_VD + 1] for o in acc]
    for s in range(n_sub):
        ot = os_[2 * s] - lam * os_[2 * s + 1]
        ms = jnp.mean(ot * ot, axis=0, keepdims=True)
        o_ref[s * ATTN_SUB:(s + 1) * ATTN_SUB, :] = (ot * lax.rsqrt(ms + LN_EPS) * g_ref[...] * (1.0 - lambda_init)).T


def _diff_attention(proj3, cos_t, sin_t, diff_lambda, subln_g, lambda_init):
    bsz, seq, _ = proj3.shape
    tq = min(ATTN_TQ, seq)
    cq, ck, cv = COL_Q // LANES, COL_K // LANES, COL_V // LANES
    return pl.pallas_call(
        functools.partial(_attn_body, lambda_init=lambda_init),
        grid=(bsz, DIFF_HEADS, seq // tq),
        in_specs=[pl.BlockSpec((None, tq, LANES), lambda b, h, i: (b, i, cq + h)),
                  pl.BlockSpec((None, seq, LANES), lambda b, h, i: (b, 0, ck + h)),
                  pl.BlockSpec((None, seq, LANES), lambda b, h, i: (b, 0, cv + h)),
                  pl.BlockSpec((tq, LANES), lambda b, h, i: (i, 0)),
                  pl.BlockSpec((tq, LANES), lambda b, h, i: (i, 0)),
                  pl.BlockSpec((seq, LANES), lambda b, h, i: (0, 0)),
                  pl.BlockSpec((seq, LANES), lambda b, h, i: (0, 0)),
                  pl.BlockSpec((4, DIFF_HD), lambda b, h, i: (0, 0)),
                  pl.BlockSpec((DIFF_VD, 1), lambda b, h, i: (0, 0))],
        out_specs=pl.BlockSpec((None, tq, LANES), lambda b, h, i: (b, i, h)),
        out_shape=jax.ShapeDtypeStruct((bsz, seq, DIFF_HEADS * DIFF_VD), F32),
        scratch_shapes=[pltpu.VMEM((seq, LANES), BF16), pltpu.VMEM((DIFF_VD + ATTN_ONES, seq), BF16)],
        name="diff_attention",
        compiler_params=_cparams(("parallel", "parallel", "arbitrary")),
    )(proj3, proj3, proj3, cos_t, sin_t, cos_t, sin_t, diff_lambda, subln_g)


DN_PAD = 8
DN_ROWS = 256
DN_BLK = 16
PAIR = 2 * DN_CHUNK
DN_UNROLL = 4


def _dn_body(q_ref, k_ref, v_ref, z_ref, gt_ref, cwq_ref, cwk_ref, cwv_ref, prm_ref, ng_ref, o_ref,
             pad_scr, q_scr, k_scr, v_scr, g_scr, dec_scr, m_scr, n_scr, qe_scr, oi_scr, *prep_scr, seq):
    c = DN_CHUNK
    n_chunks = seq // c
    set_a, set_b = prep_scr[:len(prep_scr) // 2], prep_scr[len(prep_scr) // 2:]
    ones_bd = _ones_blockdiag(LANES)
    zeros = jnp.zeros((DN_PAD, LANES), F32)
    pad_scr[0:DN_PAD, :] = zeros
    pad_scr[DN_PAD + seq:2 * DN_PAD + seq, :] = zeros
    half = (DN_CONV - 1) // 2

    def conv_silu(src_ref, cw_ref, dst_scr, normalise):
        def copy(i, carry):
            base = pl.multiple_of(i * DN_ROWS, DN_ROWS)
            pad_scr[pl.ds(base + DN_PAD, DN_ROWS), :] = src_ref[pl.ds(base, DN_ROWS), :]
            return carry

        lax.fori_loop(0, seq // DN_ROWS, copy, 0)

        def conv(i, carry):
            base = pl.multiple_of(i * DN_ROWS, DN_ROWS)
            win = pad_scr[pl.ds(base, DN_ROWS + 2 * DN_PAD), :]
            acc = win[DN_PAD - half:DN_PAD - half + DN_ROWS, :] * cw_ref[0:1, :]
            for j in range(1, DN_CONV):
                off = DN_PAD - half + j
                acc = acc + win[off:off + DN_ROWS, :] * cw_ref[j:j + 1, :]
            y = _silu(acc)
            if normalise:
                y = y * lax.rsqrt(_group_sum64(y * y, ones_bd) + 1e-6)
            dst_scr[pl.ds(base, DN_ROWS), :] = y
            return carry

        lax.fori_loop(0, seq // DN_ROWS, conv, 0)

    conv_silu(q_ref, cwq_ref, q_scr, True)
    conv_silu(k_ref, cwk_ref, k_scr, True)
    conv_silu(v_ref, cwv_ref, v_scr, False)

    a_log = prm_ref[0:1, :]
    dt_bias = prm_ref[1:2, :]

    def gates(i, carry):
        base = pl.multiple_of(i * DN_ROWS, DN_ROWS)
        blk = gt_ref[pl.ds(base, DN_ROWS), :]
        lane = lax.broadcasted_iota(jnp.int32, blk.shape, 1)
        g_scr[pl.ds(base, DN_ROWS), :] = jnp.where(lane < 4, _sigmoid(blk),
                                                   -jnp.exp(a_log) * _softplus(blk + dt_bias))
        return carry

    lax.fori_loop(0, seq // DN_ROWS, gates, 0)

    row2 = lax.broadcasted_iota(jnp.int32, (PAIR, PAIR), 0)
    col2 = lax.broadcasted_iota(jnp.int32, (PAIR, PAIR), 1)
    same_head = (row2 // c) == (col2 // c)
    same_blk = (row2 // DN_BLK) == (col2 // DN_BLK)
    lane_cl = lax.broadcasted_iota(jnp.int32, (c, LANES), 1)
    row_cl = lax.broadcasted_iota(jnp.int32, (c, LANES), 0)
    head0 = lane_cl < DN_HD

    def stack(x):
        return jnp.concatenate([jnp.where(head0, x, 0.0), jnp.where(head0, 0.0, x)], axis=0)

    def fold(x):
        return x[:c] + x[c:]

    def col_pair(x, lane0):
        return jnp.concatenate([x[:, lane0:lane0 + 1], x[:, lane0 + 1:lane0 + 2]], axis=0)

    bd_state = (lax.broadcasted_iota(jnp.int32, (LANES, LANES), 0) // DN_HD) == \
               (lax.broadcasted_iota(jnp.int32, (LANES, LANES), 1) // DN_HD)

    def phase1_chunk(n, d):
        rows = pl.ds(pl.multiple_of(n * c, c), c)
        gc = g_scr[rows, :]
        cum = gc
        sh = 1
        while sh < c:
            if d == 0:
                cum = cum + jnp.where(row_cl >= sh, pltpu.roll(cum, sh, 0), 0.0)
            else:
                cum = cum + jnp.where(row_cl < c - sh, pltpu.roll(cum, c - sh, 0), 0.0)
            sh *= 2
        cum_t = cum.T
        tot = cum[c - 1:c, :] if d == 0 else cum[0:1, :]

        kc = k_scr[rows, :]
        qc = q_scr[rows, :]
        vc = v_scr[rows, :]
        k2 = stack(kc)
        q2 = stack(qc)
        v2 = stack(vc)
        k2b = k2.astype(BF16)
        kq = _dot_nt(jnp.concatenate([k2b, q2.astype(BF16)], axis=0), k2b)
        kk = kq[:PAIR]
        qk = kq[PAIR:]
        return dict(n=n, rows=rows, gc=gc, cum=cum, cum_t=cum_t, tot=tot, k2=k2, q2=q2, v2=v2, kk=kk, qk=qk)

    def phase1_chain(ch, d):
        gc, cum, cum_t, tot = ch["gc"], ch["cum"], ch["cum_t"], ch["tot"]
        beta2 = col_pair(gc, 2 * d)
        cum2 = col_pair(cum, 4 + 2 * d)
        cum_row = jnp.concatenate([cum_t[4 + 2 * d:5 + 2 * d, :], cum_t[5 + 2 * d:6 + 2 * d, :]], axis=1)
        tot2 = jnp.concatenate([jnp.broadcast_to(tot[:, 4 + 2 * d:5 + 2 * d], (c, 1)),
                                jnp.broadcast_to(tot[:, 5 + 2 * d:6 + 2 * d], (c, 1))], axis=0)
        if d == 0:
            incl, strict = same_head & (row2 >= col2), same_head & (row2 > col2)
        else:
            incl, strict = same_head & (row2 <= col2), same_head & (row2 < col2)
        decay = jnp.exp(jnp.where(incl, cum2 - cum_row, -jnp.inf))
        lmat = jnp.where(strict, beta2 * ch["kk"] * decay, 0.0)
        rhs = jnp.concatenate([ch["v2"] * beta2, ch["k2"] * (beta2 * jnp.exp(cum2))], axis=1)
        qk_d = jnp.where(incl, ch["qk"] * decay, 0.0) * (DN_HD ** -0.5)
        qd2 = ch["q2"] * ((DN_HD ** -0.5) * jnp.exp(cum2))
        kd = fold(ch["k2"] * jnp.exp(tot2 - cum2))
        dec = jnp.where(lane_cl[0:1] < DN_HD, jnp.exp(tot[:, 4 + 2 * d:5 + 2 * d]),
                        jnp.exp(tot[:, 5 + 2 * d:6 + 2 * d]))
        return dict(d=d, n=ch["n"], rows=ch["rows"], lmat=lmat, rhs=rhs, qk_d=qk_d, qd2=qd2, kd=kd, dec=dec)

    unroll = min(DN_UNROLL, n_chunks // 2)
    n_chain = 2 * unroll

    def chunk_of(i, idx):
        m = i * unroll + idx // 2
        return idx % 2, (m if idx % 2 == 0 else n_chunks - 1 - m)

    def prepare(i, px, pz, pqk, pqd, pkd, pdec):
        for idx in range(n_chain):
            d, n = chunk_of(i, idx)
            t = phase1_chain(phase1_chunk(n, d), d)
            px[idx] = jnp.where(same_blk, -t["lmat"], 0.0).astype(BF16)
            pz[idx] = jnp.concatenate([jnp.where(same_blk, 0.0, t["lmat"]), t["rhs"]], axis=1)
            pqk[idx] = t["qk_d"].astype(BF16)
            pqd[idx] = fold(t["qd2"])
            pkd[idx] = t["kd"].astype(BF16)
            pdec[idx] = jnp.broadcast_to(t["dec"], (SUBLANES, LANES))

    def solve(i, px, pz, pqk, pqd, pkd, pdec):
        xs = [px[idx] for idx in range(n_chain)]
        zs = [pz[idx] for idx in range(n_chain)]
        bdot = lambda a, b: _dot(a.astype(BF16), b.astype(BF16))

        def apply_powers(ms, vs, n_squarings):
            for _ in range(n_squarings):
                both = [bdot(m, jnp.concatenate([m, v], axis=1)) for m, v in zip(ms, vs)]
                vs = [v + b[:, PAIR:] for v, b in zip(vs, both)]
                ms = [b[:, :PAIR] for b in both]
            return [v + bdot(m, v) for m, v in zip(ms, vs)]

        zs = apply_powers(xs, zs, int(math.log2(DN_BLK)) - 1)
        ys = [-z[:, :PAIR] for z in zs]
        rs = apply_powers(ys, [z[:, PAIR:] for z in zs], int(math.log2(DN_CHUNK // DN_BLK)) - 1)
        a_s = [_dot(pqk[idx], sol.astype(BF16)) for idx, sol in enumerate(rs)]
        mns = [_dot_tn(pkd[idx], fold(sol).astype(BF16)) for idx, sol in enumerate(rs)]
        for idx, (a, mn) in enumerate(zip(a_s, mns)):
            d, n = chunk_of(i, idx)
            rows = pl.ds(pl.multiple_of(n * c, c), c)
            n_scr[d, n] = jnp.where(bd_state, mn[:, :LANES], 0.0)
            m_scr[d, n] = jnp.where(bd_state, -mn[:, LANES:], 0.0).astype(BF16)
            qe_scr[d, rows, :] = (pqd[idx] - fold(a[:, LANES:])).astype(BF16)
            oi_scr[d, rows, :] = fold(a[:, :LANES])
            dec_scr[d, n] = pdec[idx]

    def recur(i, states, live=None):
        states = list(states)
        for idx in range(n_chain):
            d, n = chunk_of(i, idx)
            rows = pl.ds(pl.multiple_of(n * c, c), c)
            state = states[d]
            sb = state.astype(BF16)
            out = oi_scr[d, rows, :] + _dot(qe_scr[d, rows, :], sb)
            new = state * dec_scr[d, n][0:1, :] + _dot(m_scr[d, n], sb) + n_scr[d, n]
            if live is not None:
                out = jnp.where(live, out, 0.0)
                new = jnp.where(live, new, state)
            o_ref[rows, :] = o_ref[rows, :] + out
            states[d] = new
        return tuple(states)

    n_steps = n_chunks // unroll
    o_ref[...] = jnp.zeros((seq, LANES), F32)
    for idx in range(n_chain):
        d, n = chunk_of(0, idx)
        n_scr[d, n] = jnp.zeros((LANES, LANES), F32)
        m_scr[d, n] = jnp.zeros((LANES, LANES), BF16)
        dec_scr[d, n] = jnp.zeros((SUBLANES, LANES), F32)
        qe_scr[d, n * c:(n + 1) * c, :] = jnp.zeros((c, LANES), BF16)
        oi_scr[d, n * c:(n + 1) * c, :] = jnp.zeros((c, LANES), F32)
    prepare(0, *set_a)

    def phase(k, states):
        states = recur(jnp.maximum(2 * k - 1, 0), states, live=k > 0)
        solve(2 * k, *set_a)
        prepare(2 * k + 1, *set_b)
        states = recur(2 * k, states)
        solve(2 * k + 1, *set_b)
        prepare(jnp.minimum(2 * k + 2, n_steps - 1), *set_a)
        return states

    zero_state = jnp.zeros((LANES, LANES), F32)
    states = lax.fori_loop(0, n_steps // 2, phase, (zero_state, zero_state))
    recur(n_steps - 1, states)

    def finish(i, carry):
        base = pl.multiple_of(i * DN_ROWS, DN_ROWS)
        o = o_ref[pl.ds(base, DN_ROWS), :]
        ms = _group_sum64(o * o, ones_bd) * (1.0 / DN_HD)
        o_ref[pl.ds(base, DN_ROWS), :] = o * lax.rsqrt(ms + LN_EPS) * ng_ref[...] * _silu(z_ref[pl.ds(base, DN_ROWS), :])
        return carry

    lax.fori_loop(0, seq // DN_ROWS, finish, 0)


def _deltanet(proj3, dn_conv, prm, norm_g2):
    bsz, seq, _ = proj3.shape
    cq, ck, cv, cz, cg = (COL_DNQ // LANES, COL_DNK // LANES, COL_DNV // LANES, COL_DNZ // LANES, COL_GATE // LANES)
    col = lambda c0: pl.BlockSpec((None, seq, LANES), lambda b, hp: (b, 0, c0 + hp))
    cw = lambda c0: pl.BlockSpec((DN_CONV, LANES), lambda b, hp: (0, c0 + hp))
    n_chunks = seq // DN_CHUNK
    n_chain = 2 * min(DN_UNROLL, n_chunks // 2)
    prep_set = [pltpu.VMEM((n_chain, PAIR, PAIR), BF16), pltpu.VMEM((n_chain, PAIR, PAIR + 2 * LANES), F32),
                pltpu.VMEM((n_chain, PAIR, PAIR), BF16), pltpu.VMEM((n_chain, DN_CHUNK, LANES), F32),
                pltpu.VMEM((n_chain, DN_CHUNK, LANES), BF16), pltpu.VMEM((n_chain, SUBLANES, LANES), F32)]
    return pl.pallas_call(
        functools.partial(_dn_body, seq=seq),
        grid=(bsz, DN_HEADS // 2),
        in_specs=[col(cq), col(ck), col(cv), col(cz), col(cg), cw(0), cw(2), cw(4),
                  pl.BlockSpec((None, SUBLANES, LANES), lambda b, hp: (hp, 0, 0)),
                  pl.BlockSpec((1, LANES), lambda b, hp: (0, 0))],
        out_specs=pl.BlockSpec((None, seq, LANES), lambda b, hp: (b, 0, hp)),
        out_shape=jax.ShapeDtypeStruct((bsz, seq, DN_HEADS * DN_HD), F32),
        scratch_shapes=[pltpu.VMEM((seq + 2 * DN_PAD, LANES), F32),
                        pltpu.VMEM((seq, LANES), F32), pltpu.VMEM((seq, LANES), F32), pltpu.VMEM((seq, LANES), F32),
                        pltpu.VMEM((seq, LANES), F32),
                        pltpu.VMEM((2, n_chunks, SUBLANES, LANES), F32),
                        pltpu.VMEM((2, n_chunks, LANES, LANES), BF16),
                        pltpu.VMEM((2, n_chunks, LANES, LANES), F32),
                        pltpu.VMEM((2, seq, LANES), BF16),
                        pltpu.VMEM((2, seq, LANES), F32)] + prep_set + prep_set,
        name="deltanet",
        compiler_params=_cparams(("parallel", "parallel")),
    )(proj3, proj3, proj3, proj3, proj3, dn_conv, dn_conv, dn_conv, prm, norm_g2)


def _outproj_body(yc_ref, yd_ref, yn_ref, x_ref, w_ref, g_ref, b_ref, o_ref):
    mix = jnp.concatenate([yc_ref[...], yd_ref[...], yn_ref[...]], axis=-1).astype(BF16)
    h = _dot(mix, w_ref[...])
    o_ref[...] = _layer_norm(DEEPNORM_ALPHA * x_ref[...] + h, g_ref[...], b_ref[...])


def _outproj_ln(yc, yd, yn, x2, w, g, b):
    t, d = x2.shape
    tm = min(512, t)
    row = lambda n: pl.BlockSpec((tm, n), lambda i: (i, 0))
    full = lambda shape: pl.BlockSpec(shape, lambda i: (0,) * len(shape))
    return pl.pallas_call(
        _outproj_body,
        grid=(t // tm,),
        in_specs=[row(yc.shape[1]), row(yd.shape[1]), row(yn.shape[1]), row(d), full(w.shape), full((1, d)), full((1, d))],
        out_specs=row(d),
        out_shape=jax.ShapeDtypeStruct((t, d), F32),
        name="outproj_ln",
        compiler_params=_cparams(("parallel",)),
    )(yc, yd, yn, x2, w, g, b)


def _ffn_body(x_ref, w1_ref, w3_ref, w2_ref, g_ref, b_ref, o_ref):
    x = x_ref[...]
    xb = x.astype(BF16)
    h = _silu(_dot(xb, w1_ref[...])) * _dot(xb, w3_ref[...])
    f = _dot(h.astype(BF16), w2_ref[...])
    o_ref[...] = _layer_norm(DEEPNORM_ALPHA * x + f, g_ref[...], b_ref[...])


def _ffn_ln(x2, w1, w3, w2, g, b):
    t, d = x2.shape
    f = w1.shape[1]
    tm = min(512, t)
    row = pl.BlockSpec((tm, d), lambda i: (i, 0))
    once = lambda shape: pl.BlockSpec(shape, lambda i: (0,) * len(shape), pipeline_mode=pl.Buffered(1))
    return pl.pallas_call(
        _ffn_body,
        grid=(t // tm,),
        in_specs=[row, once((d, f)), once((d, f)), once((f, d)), once((1, d)), once((1, d))],
        out_specs=row,
        out_shape=jax.ShapeDtypeStruct((t, d), F32),
        name="ffn_ln",
        compiler_params=_cparams(("parallel",)),
    )(x2, w1, w3, w2, g, b)


MOE_TB = 512
MOE_RUN = SUBLANES
MOE_BUF = 2 * MOE_TB + N_EXPERTS * MOE_RUN
META_P1, META_P2, META_G1, META_G2 = range(4)


def _moe_slots(t):
    n_blocks = t // min(MOE_TB, t)
    raw = 2 * t + n_blocks * N_EXPERTS * (MOE_RUN - 1) + N_EXPERTS * (MOE_TM - 1)
    return ((raw + MOE_TM - 1) // MOE_TM) * MOE_TM


def _router_body(x_ref, rw_ref, meta_ref, cnt_ref):
    tm = x_ref.shape[0]
    xh, xl = _split_bf16(x_ref[...])
    wh, wl = _split_bf16(rw_ref[...])
    logits = _dot(xh, wh) + _dot(xl, wh) + _dot(xh, wl)
    lane = lax.broadcasted_iota(jnp.int32, logits.shape, 1)
    logits = jnp.where(lane < N_EXPERTS, logits, -jnp.inf)
    m1 = jnp.max(logits, axis=-1, keepdims=True)
    e1 = jnp.min(jnp.where(logits == m1, lane, LANES), axis=-1, keepdims=True)
    rest = jnp.where(lane == e1, -jnp.inf, logits)
    m2 = jnp.max(rest, axis=-1, keepdims=True)
    e2 = jnp.min(jnp.where(rest == m2, lane, LANES), axis=-1, keepdims=True)
    t = jnp.exp(m2 - m1)
    g1 = 1.0 / (1.0 + t)
    g2 = t / (1.0 + t)

    sel = jnp.where((lane == e1) | (lane == e2), 1.0, 0.0)
    r = lax.broadcasted_iota(jnp.int32, (tm, tm), 0)
    c = lax.broadcasted_iota(jnp.int32, (tm, tm), 1)
    strict_lower = jnp.where(r > c, 1.0, 0.0).astype(BF16)
    rank = _dot(strict_lower, sel.astype(BF16))
    cnt = jnp.sum(sel, axis=0, keepdims=True)
    run = jnp.floor((cnt + (MOE_RUN - 1.0)) * (1.0 / MOE_RUN)) * MOE_RUN
    lane1 = lane[0:1]
    start = run
    sh = 1
    while sh < N_EXPERTS:
        start = start + jnp.where(lane1 >= sh, pltpu.roll(start, sh, 1), 0.0)
        sh *= 2
    pos = rank + (start - run)
    p1 = jnp.sum(jnp.where(lane == e1, pos, 0.0), axis=-1, keepdims=True)
    p2 = jnp.sum(jnp.where(lane == e2, pos, 0.0), axis=-1, keepdims=True)
    cnt_ref[...] = jnp.broadcast_to(cnt, cnt_ref.shape)

    meta = jnp.where(lane == META_P1, p1, 0.0)
    meta = jnp.where(lane == META_P2, p2, meta)
    meta = jnp.where(lane == META_G1, g1, meta)
    meta = jnp.where(lane == META_G2, g2, meta)
    meta_ref[...] = meta


def _router(x2, rw_pad):
    t, d = x2.shape
    tm = min(MOE_TB, t)
    return pl.pallas_call(
        _router_body,
        grid=(t // tm,),
        in_specs=[pl.BlockSpec((tm, d), lambda i: (i, 0)), pl.BlockSpec((d, LANES), lambda i: (0, 0))],
        out_specs=[pl.BlockSpec((tm, LANES), lambda i: (i, 0)),
                   pl.BlockSpec((None, SUBLANES, LANES), lambda i: (i, 0, 0))],
        out_shape=[jax.ShapeDtypeStruct((t, LANES), F32),
                   jax.ShapeDtypeStruct((t // tm, SUBLANES, LANES), F32)],
        name="router",
        compiler_params=_cparams(("parallel",)),
    )(x2, rw_pad)


def _for_each_run_piece(tab_ref, n_runs, blk, fn):
    for e in range(N_EXPERTS):
        k = blk * N_EXPERTS + e
        lo, run, dst = tab_ref[k], tab_ref[n_runs + k], tab_ref[2 * n_runs + k]
        bit = MOE_TB
        while bit >= MOE_RUN:
            done = run & ~(2 * bit - 1)

            @pl.when((run & bit) != 0)
            def _():
                fn(pl.multiple_of(lo + done, MOE_RUN), pl.multiple_of(dst + done, MOE_RUN), bit)

            bit //= 2


def _dispatch_body(tab_ref, x_ref, meta_ref, xs_ref, buf, sem, *, n_runs):
    blk = pl.program_id(0)
    last = pl.num_programs(0) - 1
    cur = blk % 2
    tb = x_ref.shape[0]

    def copy(s, buf_row, sorted_row, n):
        return pltpu.make_async_copy(buf.at[s, pl.ds(buf_row, n), :], xs_ref.at[pl.ds(sorted_row, n), :], sem.at[s])

    @pl.when(blk == 0)
    def _():
        buf[0] = jnp.zeros((MOE_BUF, buf.shape[2]), F32)

        def tail(e):
            row = pl.multiple_of(tab_ref[3 * n_runs + e], MOE_TM)
            return pltpu.make_async_copy(buf.at[0, pl.ds(0, MOE_TM), :], xs_ref.at[pl.ds(row, MOE_TM), :], sem.at[0])

        for e in range(N_EXPERTS):
            pl.when(tab_ref[3 * n_runs + e] >= 0)(lambda e=e: tail(e).start())
        for e in range(N_EXPERTS):
            pl.when(tab_ref[3 * n_runs + e] >= 0)(lambda e=e: tail(e).wait())

        def unused_tile(j, carry):
            row = pl.multiple_of(j * MOE_TM, MOE_TM)
            c = pltpu.make_async_copy(buf.at[0, pl.ds(0, MOE_TM), :], xs_ref.at[pl.ds(row, MOE_TM), :], sem.at[0])
            c.start()
            c.wait()
            return carry

        lax.fori_loop(tab_ref[3 * n_runs + N_EXPERTS], xs_ref.shape[0] // MOE_TM, unused_tile, 0)

    meta_t = meta_ref[...].T
    p1 = meta_t[META_P1:META_P1 + 1, :]
    p2 = meta_t[META_P2:META_P2 + 1, :]
    slot = lax.broadcasted_iota(jnp.int32, (MOE_BUF, tb), 0).astype(F32)
    onehot = jnp.where((slot == p1) | (slot == p2), 1.0, 0.0).astype(BF16)
    buf[cur] = _dot(onehot, x_ref[...].astype(BF16))

    _for_each_run_piece(tab_ref, n_runs, blk, lambda *a: copy(cur, *a).start())

    @pl.when(blk > 0)
    def _():
        _for_each_run_piece(tab_ref, n_runs, blk - 1, lambda *a: copy(1 - cur, *a).wait())

    @pl.when(blk == last)
    def _():
        _for_each_run_piece(tab_ref, n_runs, blk, lambda *a: copy(cur, *a).wait())


def _dispatch(tab, x2, meta, n_slots):
    t, d = x2.shape
    tb = min(MOE_TB, t)
    grid_spec = pltpu.PrefetchScalarGridSpec(
        num_scalar_prefetch=1,
        grid=(t // tb,),
        in_specs=[pl.BlockSpec((tb, d), lambda i, tab: (i, 0)),
                  pl.BlockSpec((tb, LANES), lambda i, tab: (i, 0))],
        out_specs=pl.BlockSpec(memory_space=pl.ANY),
        scratch_shapes=[pltpu.VMEM((2, MOE_BUF, d), F32), pltpu.SemaphoreType.DMA((2,))],
    )
    return pl.pallas_call(
        functools.partial(_dispatch_body, n_runs=(t // tb) * N_EXPERTS),
        grid_spec=grid_spec,
        out_shape=jax.ShapeDtypeStruct((n_slots, d), F32),
        name="moe_dispatch",
        compiler_params=pltpu.CompilerParams(dimension_semantics=("arbitrary",), vmem_limit_bytes=VMEM_LIMIT,
                                             has_side_effects=True),
    )(tab, x2, meta)


def _experts_body(te_ref, nu_ref, xs_ref, w1_ref, w3_ref, w2_ref, ys_ref, xb_scr, acc_scr):
    i = pl.program_id(0)
    f = pl.program_id(1)
    used = i < nu_ref[0]

    @pl.when(used & (f == 0))
    def _():
        xb_scr[...] = xs_ref[...].astype(BF16)

    @pl.when(used)
    def _():
        xb = xb_scr[...]
        h = _silu(_dot(xb, w1_ref[...])) * _dot(xb, w3_ref[...])
        part = _dot(h.astype(BF16), w2_ref[...])

        @pl.when(f == 0)
        def _():
            acc_scr[...] = part

        @pl.when(f > 0)
        def _():
            acc_scr[...] = acc_scr[...] + part

    @pl.when(f == pl.num_programs(1) - 1)
    def _():
        @pl.when(used)
        def _():
            ys_ref[...] = acc_scr[...]

        @pl.when(jnp.logical_not(used))
        def _():
            ys_ref[...] = jnp.zeros_like(ys_ref)


def _experts(tile_expert, n_used, xs, w1, w3, w2):
    n_slots, d = xs.shape
    n_tiles = n_slots // MOE_TM
    nf = w1.shape[2] // MOE_TF

    def fidx(i, f, te, nu):
        return jnp.where(i < nu[0], f, nf - 1)

    grid_spec = pltpu.PrefetchScalarGridSpec(
        num_scalar_prefetch=2,
        grid=(n_tiles, nf),
        in_specs=[pl.BlockSpec((MOE_TM, d), lambda i, f, te, nu: (jnp.minimum(i, nu[0] - 1), 0)),
                  pl.BlockSpec((None, d, MOE_TF), lambda i, f, te, nu: (te[i], 0, fidx(i, f, te, nu))),
                  pl.BlockSpec((None, d, MOE_TF), lambda i, f, te, nu: (te[i], 0, fidx(i, f, te, nu))),
                  pl.BlockSpec((None, MOE_TF, d), lambda i, f, te, nu: (te[i], fidx(i, f, te, nu), 0))],
        out_specs=pl.BlockSpec((MOE_TM, d), lambda i, f, te, nu: (i, 0)),
        scratch_shapes=[pltpu.VMEM((MOE_TM, d), BF16), pltpu.VMEM((MOE_TM, d), F32)],
    )
    return pl.pallas_call(
        _experts_body,
        grid_spec=grid_spec,
        out_shape=jax.ShapeDtypeStruct((n_slots, d), F32),
        name="moe_experts",
        compiler_params=_cparams(("arbitrary", "arbitrary")),
    )(tile_expert, n_used, xs, w1, w3, w2)


def _split_bf16(v):
    hi = v.astype(BF16)
    return hi, (v - hi.astype(F32)).astype(BF16)


def _combine_body(tab_ref, x_ref, meta_ref, ys_ref, g_ref, b_ref, o_ref, ybuf, sem, *, n_runs):
    blk = pl.program_id(0)
    last = pl.num_programs(0) - 1
    cur = blk % 2
    tb = x_ref.shape[0]

    def copy(s, buf_row, sorted_row, n):
        return pltpu.make_async_copy(ys_ref.at[pl.ds(sorted_row, n), :], ybuf.at[s, pl.ds(buf_row, n), :], sem.at[s])

    @pl.when(blk == 0)
    def _():
        ybuf[...] = jnp.zeros_like(ybuf)
        _for_each_run_piece(tab_ref, n_runs, blk, lambda *a: copy(cur, *a).start())

    @pl.when(blk < last)
    def _():
        _for_each_run_piece(tab_ref, n_runs, blk + 1, lambda *a: copy(1 - cur, *a).start())

    _for_each_run_piece(tab_ref, n_runs, blk, lambda *a: copy(cur, *a).wait())

    meta = meta_ref[...]
    p1 = meta[:, META_P1:META_P1 + 1]
    p2 = meta[:, META_P2:META_P2 + 1]
    g1 = meta[:, META_G1:META_G1 + 1]
    g2 = meta[:, META_G2:META_G2 + 1]
    slot = lax.broadcasted_iota(jnp.int32, (tb, MOE_BUF), 1).astype(F32)
    weights = jnp.where(slot == p1, g1, 0.0) + jnp.where(slot == p2, g2, 0.0)
    wh, wl = _split_bf16(weights)
    yh, yl = _split_bf16(ybuf[cur])
    f = _dot(wh, yh) + _dot(wh, yl) + _dot(wl, yh)
    o_ref[...] = _layer_norm(DEEPNORM_ALPHA * x_ref[...] + f, g_ref[...], b_ref[...])


def _combine_ln(tab, x2, meta, ys, g, b):
    t, d = x2.shape
    tb = min(MOE_TB, t)
    grid_spec = pltpu.PrefetchScalarGridSpec(
        num_scalar_prefetch=1,
        grid=(t // tb,),
        in_specs=[pl.BlockSpec((tb, d), lambda i, tab: (i, 0)),
                  pl.BlockSpec((tb, LANES), lambda i, tab: (i, 0)),
                  pl.BlockSpec(memory_space=pl.ANY),
                  pl.BlockSpec((1, d), lambda i, tab: (0, 0)),
                  pl.BlockSpec((1, d), lambda i, tab: (0, 0))],
        out_specs=pl.BlockSpec((tb, d), lambda i, tab: (i, 0)),
        scratch_shapes=[pltpu.VMEM((2, MOE_BUF, d), F32), pltpu.SemaphoreType.DMA((2,))],
    )
    return pl.pallas_call(
        functools.partial(_combine_body, n_runs=(t // tb) * N_EXPERTS),
        grid_spec=grid_spec,
        out_shape=jax.ShapeDtypeStruct((t, d), F32),
        name="moe_combine_ln",
        compiler_params=_cparams(("arbitrary",)),
    )(tab, x2, meta, ys, g, b)


def _moe_ln(x2, rw_pad, w1, w3, w2, g, b):
    t, d = x2.shape
    meta, blk_cnt = _router(x2, rw_pad)
    cnt = blk_cnt[:, 0, :N_EXPERTS].astype(jnp.int32)
    run = ((cnt + MOE_RUN - 1) // MOE_RUN) * MOE_RUN
    lo = jnp.cumsum(run, axis=1) - run
    before = jnp.cumsum(run, axis=0) - run
    padded = ((jnp.sum(run, axis=0) + MOE_TM - 1) // MOE_TM) * MOE_TM
    ends = jnp.cumsum(padded)
    starts = ends - padded
    tails = jnp.where(padded > 0, ends - MOE_TM, -1)
    n_slots = _moe_slots(t)
    n_tiles = n_slots // MOE_TM
    n_used = (ends[-1] // MOE_TM).astype(jnp.int32)
    tab = jnp.concatenate([lo.reshape(-1), run.reshape(-1), (starts[None, :] + before).reshape(-1),
                           tails, n_used.reshape(1)]).astype(jnp.int32)
    tile_start = jnp.arange(n_tiles, dtype=jnp.int32) * MOE_TM
    tile_expert = jnp.sum((tile_start[:, None] >= ends[None, :]).astype(jnp.int32), axis=1)
    last_expert = jnp.sum((((n_used - 1) * MOE_TM) >= ends).astype(jnp.int32))
    tile_expert = jnp.where(jnp.arange(n_tiles) < n_used, tile_expert, last_expert).astype(jnp.int32)

    xs = _dispatch(tab, x2, meta, n_slots)
    ys = _experts(tile_expert, n_used.reshape(1), xs, w1, w3, w2)
    return _combine_ln(tab, x2, meta, ys, g, b)


CAST_ROWS = 512


def _cast_body(w_ref, o_ref):
    o_ref[...] = w_ref[...].astype(BF16)


def _expert_weights_bf16(w, layer):
    _, n_exp, rows, cols = w.shape
    return pl.pallas_call(
        _cast_body,
        grid=(n_exp, rows // CAST_ROWS),
        in_specs=[pl.BlockSpec((None, None, CAST_ROWS, cols), lambda e, r: (layer, e, r, 0))],
        out_specs=pl.BlockSpec((None, CAST_ROWS, cols), lambda e, r: (e, r, 0)),
        out_shape=jax.ShapeDtypeStruct((n_exp, rows, cols), BF16),
        name="expert_weight_cast",
        compiler_params=_cparams(("parallel", "parallel")),
    )(w)
def _rope_tables(seq):
    inv = ROPE_THETA ** (-jnp.arange(0, DIFF_HD, 2, dtype=F32) / DIFF_HD)
    ang = jnp.arange(seq, dtype=F32)[:, None] * inv[None, :]
    cos, sin = jnp.cos(ang), jnp.sin(ang)
    cos_t = jnp.tile(cos, (1, LANES // cos.shape[1]))
    sin_t = jnp.tile(jnp.concatenate([-sin, sin], axis=1), (1, LANES // (2 * sin.shape[1])))
    return cos_t, sin_t


def _gate_columns():
    cols = []
    for hp in range(DN_HEADS // 2):
        blk = [PROJ_RAW - 16 + d * DN_HEADS + 2 * hp + hl for d in range(2) for hl in range(2)]
        blk += [PROJ_RAW - 8 + d * DN_HEADS + 2 * hp + hl for d in range(2) for hl in range(2)]
        cols.append(blk)
    return cols


def _prep_w_in(w_in_l):
    parts = [w_in_l[:, :COL_GATE]]
    for blk in _gate_columns():
        parts.append(w_in_l[:, jnp.array(blk)])
        parts.append(jnp.zeros((w_in_l.shape[0], LANES - len(blk)), w_in_l.dtype))
    return jnp.concatenate(parts, axis=1).astype(BF16)


def _prep_dn_params(a_log_l, dt_bias_l):
    out = []
    for hp in range(DN_HEADS // 2):
        idx = [(d, 2 * hp + hl) for d in range(2) for hl in range(2)]
        a = jnp.stack([a_log_l[d, h] for d, h in idx])
        t = jnp.stack([dt_bias_l[d, h] for d, h in idx])
        blk = jnp.zeros((SUBLANES, LANES), F32)
        blk = blk.at[0, 4:8].set(a).at[1, 4:8].set(t)
        out.append(blk)
    return jnp.stack(out)


def kernel(x, w_in, w_o, ln1_g, ln1_b, ln2_g, ln2_b, conv_dw, conv_dw_b, conv_ln_g, conv_ln_b, conv_pw,
           diff_lambda, diff_subln_g, dn_conv, dn_a_log, dn_dt_bias, dn_norm_g,
           ffn_w1, ffn_w3, ffn_w2, router_w, moe_w1, moe_w3, moe_w2):
    bsz, seq, d = x.shape
    t = bsz * seq
    cos_t, sin_t = _rope_tables(seq)
    x2 = x.reshape(t, d)
    row = lambda v: v.reshape(1, -1)
    for layer in range(DEPTH):
        lambda_init = 0.8 - 0.6 * math.exp(-0.3 * layer)
        proj = _inproj(x2, _prep_w_in(w_in[layer]))
        proj3 = proj.reshape(bsz, seq, PROJ_PAD)
        y_conv = _conv_module(proj3, conv_dw[layer], row(conv_dw_b[layer]), row(conv_ln_g[layer]),
                              row(conv_ln_b[layer]), conv_pw[layer].astype(BF16))
        y_diff = _diff_attention(proj3, cos_t, sin_t, diff_lambda[layer], diff_subln_g[layer].reshape(-1, 1),
                                 lambda_init)
        y_dn = _deltanet(proj3, dn_conv[layer], _prep_dn_params(dn_a_log[layer], dn_dt_bias[layer]),
                         row(jnp.tile(dn_norm_g[layer], 2)))
        x2 = _outproj_ln(y_conv.reshape(t, -1), y_diff.reshape(t, -1), y_dn.reshape(t, -1), x2,
                         w_o[layer].astype(BF16), row(ln1_g[layer]), row(ln1_b[layer]))
        j = layer // 2
        if layer % 2 == 0:
            x2 = _ffn_ln(x2, ffn_w1[j].astype(BF16), ffn_w3[j].astype(BF16), ffn_w2[j].astype(BF16),
                         row(ln2_g[layer]), row(ln2_b[layer]))
        else:
            rw_pad = jnp.pad(router_w[j], ((0, 0), (0, LANES - N_EXPERTS)))
            x2 = _moe_ln(x2, rw_pad, _expert_weights_bf16(moe_w1, j), _expert_weights_bf16(moe_w3, j),
                         _expert_weights_bf16(moe_w2, j), row(ln2_g[layer]), row(ln2_b[layer]))
    return x2.reshape(bsz, seq, d)
```

```python
import functools
import math

import jax
import jax.numpy as jnp
from jax import lax
from jax.experimental import pallas as pl
from jax.experimental.pallas import tpu as pltpu

F32 = jnp.float32
BF16 = jnp.bfloat16

D_MODEL = 1024
DEPTH = 4
CONV_W = D_MODEL // 4
CONV_WIDTH = 31
DIFF_HEADS = 4
DIFF_HD = D_MODEL // 16
DIFF_VD = 2 * DIFF_HD
DN_HEADS = 4
DN_HD = D_MODEL // 16
DN_CONV = 5
DN_CHUNK = 64
ROPE_THETA = 10000.0
D_FF = 11 * D_MODEL // 4
N_EXPERTS = 8
D_FF_EXPERT = 7 * D_MODEL // 2
DEEPNORM_ALPHA = (2 * DEPTH) ** 0.25
LN_EPS = 1e-5

LANES = 128
SUBLANES = 8
VMEM_LIMIT = 56 * 2 ** 20

COL_CONV = 0
COL_Q = 512
COL_K = 1024
COL_V = 1536
COL_DNQ = 2048
COL_DNK = 2304
COL_DNV = 2560
COL_DNZ = 2816
COL_GATE = 3072
PROJ_RAW = 3088
PROJ_PAD = COL_GATE + 2 * LANES

MOE_TM = 512
MOE_TF = 1792


def _cparams(sem):
    return pltpu.CompilerParams(dimension_semantics=sem, vmem_limit_bytes=VMEM_LIMIT)


def _sigmoid(x):
    return 1.0 / (1.0 + jnp.exp(-x))


def _silu(x):
    return x * _sigmoid(x)


def _softplus(x):
    return jnp.maximum(x, 0.0) + jnp.log1p(jnp.exp(-jnp.abs(x)))


def _layer_norm(x, g, b):
    mu = jnp.mean(x, axis=-1, keepdims=True)
    xc = x - mu
    var = jnp.mean(xc * xc, axis=-1, keepdims=True)
    return xc * lax.rsqrt(var + LN_EPS) * g + b


def _dot(a, b):
    return jnp.dot(a, b, preferred_element_type=F32)


def _dot_nt(a, b):
    return lax.dot_general(a, b, (((1,), (1,)), ((), ())), preferred_element_type=F32)


def _dot_tn(a, b):
    return lax.dot_general(a, b, (((0,), (0,)), ((), ())), preferred_element_type=F32)


def _group_sum64(x, ones_bd):
    hi = x.astype(BF16)
    lo = (x - hi.astype(F32)).astype(BF16)
    return _dot(hi, ones_bd) + _dot(lo, ones_bd)


def _ones_blockdiag(n):
    r = lax.broadcasted_iota(jnp.int32, (n, n), 0) // DN_HD
    c = lax.broadcasted_iota(jnp.int32, (n, n), 1) // DN_HD
    return jnp.where(r == c, 1.0, 0.0).astype(BF16)


def _inproj_body(x_ref, w_ref, o_ref):
    o_ref[...] = _dot(x_ref[...].astype(BF16), w_ref[...])


def _inproj(x2, w):
    t, d = x2.shape
    n = w.shape[1]
    tm = min(512, t)
    return pl.pallas_call(
        _inproj_body,
        grid=(t // tm,),
        in_specs=[pl.BlockSpec((tm, d), lambda i: (i, 0)),
                  pl.BlockSpec((d, n), lambda i: (0, 0))],
        out_specs=pl.BlockSpec((tm, n), lambda i: (i, 0)),
        out_shape=jax.ShapeDtypeStruct((t, n), F32),
        name="inproj",
        compiler_params=_cparams(("parallel",)),
    )(x2, w)


CONV_PAD = 16
CONV_ROWS = 128


def _conv_body(p_ref, dw_ref, dwb_ref, g_ref, b_ref, pw_ref, o_ref, pad_scr, *, seq):
    zeros = jnp.zeros((CONV_PAD, CONV_W), F32)
    pad_scr[0:CONV_PAD, :] = zeros
    pad_scr[CONV_PAD + seq:2 * CONV_PAD + seq, :] = zeros

    def glu(i, carry):
        base = pl.multiple_of(i * CONV_ROWS, CONV_ROWS)
        p = p_ref[pl.ds(base, CONV_ROWS), :]
        pad_scr[pl.ds(base + CONV_PAD, CONV_ROWS), :] = p[:, :CONV_W] * _sigmoid(p[:, CONV_W:])
        return carry

    lax.fori_loop(0, seq // CONV_ROWS, glu, 0)
    half = (CONV_WIDTH - 1) // 2

    def conv(i, carry):
        base = pl.multiple_of(i * CONV_ROWS, CONV_ROWS)
        acc = jnp.zeros((CONV_ROWS, CONV_W), F32) + dwb_ref[...]
        win = pad_scr[pl.ds(base, CONV_ROWS + 2 * CONV_PAD), :]
        n_win = CONV_ROWS + 2 * CONV_PAD
        for r in range(SUBLANES):
            wr = win if r == 0 else pltpu.roll(win, n_win - r, 0)
            for j in range(CONV_WIDTH):
                off = CONV_PAD - half + j
                if off % SUBLANES == r:
                    acc = acc + wr[off - r:off - r + CONV_ROWS, :] * dw_ref[j:j + 1, :]
        y = _silu(_layer_norm(acc, g_ref[...], b_ref[...]))
        o_ref[pl.ds(base, CONV_ROWS), :] = _dot(y.astype(BF16), pw_ref[...])
        return carry

    lax.fori_loop(0, seq // CONV_ROWS, conv, 0)


def _conv_module(proj3, dw, dwb, g, b, pw):
    bsz, seq, _ = proj3.shape
    full = lambda shape: pl.BlockSpec(shape, lambda i: (0,) * len(shape))
    return pl.pallas_call(
        functools.partial(_conv_body, seq=seq),
        grid=(bsz,),
        in_specs=[pl.BlockSpec((None, seq, 2 * CONV_W), lambda i: (i, 0, COL_CONV // (2 * CONV_W))),
                  full((CONV_WIDTH, CONV_W)), full((1, CONV_W)), full((1, CONV_W)), full((1, CONV_W)),
                  full((CONV_W, CONV_W))],
        out_specs=pl.BlockSpec((None, seq, CONV_W), lambda i: (i, 0, 0)),
        out_shape=jax.ShapeDtypeStruct((bsz, seq, CONV_W), F32),
        scratch_shapes=[pltpu.VMEM((seq + 2 * CONV_PAD, CONV_W), F32)],
        name="conv_module",
        compiler_params=_cparams(("parallel",)),
    )(proj3, dw, dwb, g, b, pw)


ATTN_TQ = 2048
ATTN_SUB = 256
ATTN_KB = 512
ATTN_ONES = 16
ATTN_FOLD = 64


def _rope(x, cos, sin_signed):
    lane = lax.broadcasted_iota(jnp.int32, x.shape, 1)
    first = (lane % DIFF_HD) < (DIFF_HD // 2)
    rot = jnp.where(first, pltpu.roll(x, LANES - DIFF_HD // 2, 1), pltpu.roll(x, DIFF_HD // 2, 1))
    return x * cos + rot * sin_signed


def _attn_body(q_ref, k_ref, v_ref, cq_ref, sq_ref, ck_ref, sk_ref, dl_ref, g_ref, o_ref, kr_scr, vt_scr,
               *, lambda_init):
    @pl.when(pl.program_id(2) == 0)
    def _():
        kr_scr[...] = _rope(k_ref[...], ck_ref[...], sk_ref[...]).astype(BF16)
        vt_scr[0:DIFF_VD, :] = v_ref[...].T.astype(BF16)
        vt_scr[DIFF_VD:, :] = jnp.ones((ATTN_ONES, vt_scr.shape[1]), BF16)

    dl = dl_ref[...]
    lam = (jnp.exp(jnp.sum(dl[0:1] * dl[1:2], axis=-1, keepdims=True))
           - jnp.exp(jnp.sum(dl[2:3] * dl[3:4], axis=-1, keepdims=True)) + lambda_init)

    q = _rope(q_ref[...], cq_ref[...], sq_ref[...]) * (DIFF_HD ** -0.5 * math.log2(math.e))
    lane = lax.broadcasted_iota(jnp.int32, q.shape, 1)
    first_map = lane < DIFF_HD
    kr = kr_scr[...]
    vt = vt_scr[...]

    def fold_keys(x, op):
        part = op(x.reshape(x.shape[0] // ATTN_FOLD, ATTN_FOLD, x.shape[1]), axis=0)
        return op(part, axis=0, keepdims=True)

    n_sub = q.shape[0] // ATTN_SUB
    qms = [jnp.where(first_map if mp == 0 else jnp.logical_not(first_map), q, 0.0)[s * ATTN_SUB:(s + 1) * ATTN_SUB]
           for s in range(n_sub) for mp in range(2)]
    qbs = [qm.astype(BF16) for qm in qms]
    kb = min(ATTN_KB, kr.shape[0])
    m_run = acc = None
    for j in range(kr.shape[0] // kb):
        krj = kr[j * kb:(j + 1) * kb, :]
        vtj = vt[:, j * kb:(j + 1) * kb]
        sts = [_dot_nt(krj, qb) for qb in qbs]
        ms_ = [fold_keys(st, jnp.max) for st in sts]
        if j > 0:
            ms_ = [jnp.maximum(m, mo) for m, mo in zip(ms_, m_run)]
        ps = [jnp.exp2(st - m).astype(BF16) for st, m in zip(sts, ms_)]
        pv = [_dot(vtj, p) for p in ps]
        if j > 0:
            pv = [o + a * jnp.exp2(mo - m) for o, a, mo, m in zip(pv, acc, m_run, ms_)]
        m_run, acc = ms_, pv
    os_ = [o[0:DIFF_VD] / o[DIFF_VD:DIFF_VD + 1] for o in acc]
    for s in range(n_sub):
        ot = os_[2 * s] - lam * os_[2 * s + 1]
        ms = jnp.mean(ot * ot, axis=0, keepdims=True)
        o_ref[s * ATTN_SUB:(s + 1) * ATTN_SUB, :] = (ot * lax.rsqrt(ms + LN_EPS) * g_ref[...] * (1.0 - lambda_init)).T


def _diff_attention(proj3, cos_t, sin_t, diff_lambda, subln_g, lambda_init):
    bsz, seq, _ = proj3.shape
    tq = min(ATTN_TQ, seq)
    cq, ck, cv = COL_Q // LANES, COL_K // LANES, COL_V // LANES
    return pl.pallas_call(
        functools.partial(_attn_body, lambda_init=lambda_init),
        grid=(bsz, DIFF_HEADS, seq // tq),
        in_specs=[pl.BlockSpec((None, tq, LANES), lambda b, h, i: (b, i, cq + h)),
                  pl.BlockSpec((None, seq, LANES), lambda b, h, i: (b, 0, ck + h)),
                  pl.BlockSpec((None, seq, LANES), lambda b, h, i: (b, 0, cv + h)),
                  pl.BlockSpec((tq, LANES), lambda b, h, i: (i, 0)),
                  pl.BlockSpec((tq, LANES), lambda b, h, i: (i, 0)),
                  pl.BlockSpec((seq, LANES), lambda b, h, i: (0, 0)),
                  pl.BlockSpec((seq, LANES), lambda b, h, i: (0, 0)),
                  pl.BlockSpec((4, DIFF_HD), lambda b, h, i: (0, 0)),
                  pl.BlockSpec((DIFF_VD, 1), lambda b, h, i: (0, 0))],
        out_specs=pl.BlockSpec((None, tq, LANES), lambda b, h, i: (b, i, h)),
        out_shape=jax.ShapeDtypeStruct((bsz, seq, DIFF_HEADS * DIFF_VD), F32),
        scratch_shapes=[pltpu.VMEM((seq, LANES), BF16), pltpu.VMEM((DIFF_VD + ATTN_ONES, seq), BF16)],
        name="diff_attention",
        compiler_params=_cparams(("parallel", "parallel", "arbitrary")),
    )(proj3, proj3, proj3, cos_t, sin_t, cos_t, sin_t, diff_lambda, subln_g)


DN_PAD = 8
DN_ROWS = 256
DN_BLK = 8
PAIR = 2 * DN_CHUNK
DN_UNROLL = 4


def _dn_body(q_ref, k_ref, v_ref, z_ref, gt_ref, cwq_ref, cwk_ref, cwv_ref, prm_ref, ng_ref, o_ref,
             pad_scr, q_scr, k_scr, v_scr, g_scr, dec_scr, m_scr, n_scr, qe_scr, oi_scr, *prep_scr, seq):
    c = DN_CHUNK
    n_chunks = seq // c
    set_a, set_b = prep_scr[:len(prep_scr) // 2], prep_scr[len(prep_scr) // 2:]
    ones_bd = _ones_blockdiag(LANES)
    zeros = jnp.zeros((DN_PAD, LANES), F32)
    pad_scr[0:DN_PAD, :] = zeros
    pad_scr[DN_PAD + seq:2 * DN_PAD + seq, :] = zeros
    half = (DN_CONV - 1) // 2

    def conv_silu(src_ref, cw_ref, dst_scr, normalise):
        def copy(i, carry):
            base = pl.multiple_of(i * DN_ROWS, DN_ROWS)
            pad_scr[pl.ds(base + DN_PAD, DN_ROWS), :] = src_ref[pl.ds(base, DN_ROWS), :]
            return carry

        lax.fori_loop(0, seq // DN_ROWS, copy, 0)

        def conv(i, carry):
            base = pl.multiple_of(i * DN_ROWS, DN_ROWS)
            win = pad_scr[pl.ds(base, DN_ROWS + 2 * DN_PAD), :]
            acc = win[DN_PAD - half:DN_PAD - half + DN_ROWS, :] * cw_ref[0:1, :]
            for j in range(1, DN_CONV):
                off = DN_PAD - half + j
                acc = acc + win[off:off + DN_ROWS, :] * cw_ref[j:j + 1, :]
            y = _silu(acc)
            if normalise:
                y = y * lax.rsqrt(_group_sum64(y * y, ones_bd) + 1e-6)
            dst_scr[pl.ds(base, DN_ROWS), :] = y
            return carry

        lax.fori_loop(0, seq // DN_ROWS, conv, 0)

    conv_silu(q_ref, cwq_ref, q_scr, True)
    conv_silu(k_ref, cwk_ref, k_scr, True)
    conv_silu(v_ref, cwv_ref, v_scr, False)

    a_log = prm_ref[0:1, :]
    dt_bias = prm_ref[1:2, :]

    def gates(i, carry):
        base = pl.multiple_of(i * DN_ROWS, DN_ROWS)
        blk = gt_ref[pl.ds(base, DN_ROWS), :]
        lane = lax.broadcasted_iota(jnp.int32, blk.shape, 1)
        g_scr[pl.ds(base, DN_ROWS), :] = jnp.where(lane < 4, _sigmoid(blk),
                                                   -jnp.exp(a_log) * _softplus(blk + dt_bias))
        return carry

    lax.fori_loop(0, seq // DN_ROWS, gates, 0)

    row2 = lax.broadcasted_iota(jnp.int32, (PAIR, PAIR), 0)
    col2 = lax.broadcasted_iota(jnp.int32, (PAIR, PAIR), 1)
    same_head = (row2 // c) == (col2 // c)
    same_blk = (row2 // DN_BLK) == (col2 // DN_BLK)
    lane_cl = lax.broadcasted_iota(jnp.int32, (c, LANES), 1)
    row_cl = lax.broadcasted_iota(jnp.int32, (c, LANES), 0)
    head0 = lane_cl < DN_HD

    def stack(x):
        return jnp.concatenate([jnp.where(head0, x, 0.0), jnp.where(head0, 0.0, x)], axis=0)

    def fold(x):
        return x[:c] + x[c:]

    def col_pair(x, lane0):
        return jnp.concatenate([x[:, lane0:lane0 + 1], x[:, lane0 + 1:lane0 + 2]], axis=0)

    bd_state = (lax.broadcasted_iota(jnp.int32, (LANES, LANES), 0) // DN_HD) == \
               (lax.broadcasted_iota(jnp.int32, (LANES, LANES), 1) // DN_HD)

    def phase1_chunk(n, d):
        rows = pl.ds(pl.multiple_of(n * c, c), c)
        gc = g_scr[rows, :]
        cum = gc
        sh = 1
        while sh < c:
            if d == 0:
                cum = cum + jnp.where(row_cl >= sh, pltpu.roll(cum, sh, 0), 0.0)
            else:
                cum = cum + jnp.where(row_cl < c - sh, pltpu.roll(cum, c - sh, 0), 0.0)
            sh *= 2
        cum_t = cum.T
        tot = cum[c - 1:c, :] if d == 0 else cum[0:1, :]

        kc = k_scr[rows, :]
        qc = q_scr[rows, :]
        vc = v_scr[rows, :]
        k2 = stack(kc)
        q2 = stack(qc)
        v2 = stack(vc)
        k2b = k2.astype(BF16)
        kq = _dot_nt(jnp.concatenate([k2b, q2.astype(BF16)], axis=0), k2b)
        kk = kq[:PAIR]
        qk = kq[PAIR:]
        return dict(n=n, rows=rows, gc=gc, cum=cum, cum_t=cum_t, tot=tot, k2=k2, q2=q2, v2=v2, kk=kk, qk=qk)

    def phase1_chain(ch, d):
        gc, cum, cum_t, tot = ch["gc"], ch["cum"], ch["cum_t"], ch["tot"]
        beta2 = col_pair(gc, 2 * d)
        cum2 = col_pair(cum, 4 + 2 * d)
        cum_row = jnp.concatenate([cum_t[4 + 2 * d:5 + 2 * d, :], cum_t[5 + 2 * d:6 + 2 * d, :]], axis=1)
        tot2 = jnp.concatenate([jnp.broadcast_to(tot[:, 4 + 2 * d:5 + 2 * d], (c, 1)),
                                jnp.broadcast_to(tot[:, 5 + 2 * d:6 + 2 * d], (c, 1))], axis=0)
        if d == 0:
            incl, strict = same_head & (row2 >= col2), same_head & (row2 > col2)
        else:
            incl, strict = same_head & (row2 <= col2), same_head & (row2 < col2)
        decay = jnp.exp(jnp.where(incl, cum2 - cum_row, -jnp.inf))
        lmat = jnp.where(strict, beta2 * ch["kk"] * decay, 0.0)
        rhs = jnp.concatenate([ch["v2"] * beta2, ch["k2"] * (beta2 * jnp.exp(cum2))], axis=1)
        qk_d = jnp.where(incl, ch["qk"] * decay, 0.0) * (DN_HD ** -0.5)
        qd2 = ch["q2"] * ((DN_HD ** -0.5) * jnp.exp(cum2))
        kd = fold(ch["k2"] * jnp.exp(tot2 - cum2))
        dec = jnp.where(lane_cl[0:1] < DN_HD, jnp.exp(tot[:, 4 + 2 * d:5 + 2 * d]),
                        jnp.exp(tot[:, 5 + 2 * d:6 + 2 * d]))
        return dict(d=d, n=ch["n"], rows=ch["rows"], lmat=lmat, rhs=rhs, qk_d=qk_d, qd2=qd2, kd=kd, dec=dec)

    unroll = min(DN_UNROLL, n_chunks // 2)
    n_chain = 2 * unroll

    def chunk_of(i, idx):
        m = i * unroll + idx // 2
        return idx % 2, (m if idx % 2 == 0 else n_chunks - 1 - m)

    def prepare(i, px, pz, pqk, pqd, pkd, pdec):
        for idx in range(n_chain):
            d, n = chunk_of(i, idx)
            t = phase1_chain(phase1_chunk(n, d), d)
            px[idx] = jnp.where(same_blk, -t["lmat"], 0.0).astype(BF16)
            pz[idx] = jnp.concatenate([jnp.where(same_blk, 0.0, t["lmat"]), t["rhs"]], axis=1)
            pqk[idx] = t["qk_d"].astype(BF16)
            pqd[idx] = fold(t["qd2"])
            pkd[idx] = t["kd"].astype(BF16)
            pdec[idx] = jnp.broadcast_to(t["dec"], (SUBLANES, LANES))

    def solve(i, px, pz, pqk, pqd, pkd, pdec):
        xs = [px[idx] for idx in range(n_chain)]
        zs = [pz[idx] for idx in range(n_chain)]
        bdot = lambda a, b: _dot(a.astype(BF16), b.astype(BF16))

        def apply_powers(ms, vs, n_squarings):
            for _ in range(n_squarings):
                both = [bdot(m, jnp.concatenate([m, v], axis=1)) for m, v in zip(ms, vs)]
                vs = [v + b[:, PAIR:] for v, b in zip(vs, both)]
                ms = [b[:, :PAIR] for b in both]
            return [v + bdot(m, v) for m, v in zip(ms, vs)]

        zs = apply_powers(xs, zs, int(math.log2(DN_BLK)) - 1)
        ys = [-z[:, :PAIR] for z in zs]
        rs = apply_powers(ys, [z[:, PAIR:] for z in zs], int(math.log2(DN_CHUNK // DN_BLK)) - 1)
        a_s = [_dot(pqk[idx], sol.astype(BF16)) for idx, sol in enumerate(rs)]
        mns = [_dot_tn(pkd[idx], fold(sol).astype(BF16)) for idx, sol in enumerate(rs)]
        for idx, (a, mn) in enumerate(zip(a_s, mns)):
            d, n = chunk_of(i, idx)
            rows = pl.ds(pl.multiple_of(n * c, c), c)
            n_scr[d, n] = jnp.where(bd_state, mn[:, :LANES], 0.0)
            m_scr[d, n] = jnp.where(bd_state, -mn[:, LANES:], 0.0).astype(BF16)
            qe_scr[d, rows, :] = (pqd[idx] - fold(a[:, LANES:])).astype(BF16)
            oi_scr[d, rows, :] = fold(a[:, :LANES])
            dec_scr[d, n] = pdec[idx]

    def recur(i, states, live=None):
        states = list(states)
        for idx in range(n_chain):
            d, n = chunk_of(i, idx)
            rows = pl.ds(pl.multiple_of(n * c, c), c)
            state = states[d]
            sb = state.astype(BF16)
            out = oi_scr[d, rows, :] + _dot(qe_scr[d, rows, :], sb)
            new = state * dec_scr[d, n][0:1, :] + _dot(m_scr[d, n], sb) + n_scr[d, n]
            if live is not None:
                out = jnp.where(live, out, 0.0)
                new = jnp.where(live, new, state)
            o_ref[rows, :] = o_ref[rows, :] + out
            states[d] = new
        return tuple(states)

    n_steps = n_chunks // unroll
    o_ref[...] = jnp.zeros((seq, LANES), F32)
    for idx in range(n_chain):
        d, n = chunk_of(0, idx)
        n_scr[d, n] = jnp.zeros((LANES, LANES), F32)
        m_scr[d, n] = jnp.zeros((LANES, LANES), BF16)
        dec_scr[d, n] = jnp.zeros((SUBLANES, LANES), F32)
        qe_scr[d, n * c:(n + 1) * c, :] = jnp.zeros((c, LANES), BF16)
        oi_scr[d, n * c:(n + 1) * c, :] = jnp.zeros((c, LANES), F32)
    prepare(0, *set_a)

    def phase(k, states):
        states = recur(jnp.maximum(2 * k - 1, 0), states, live=k > 0)
        solve(2 * k, *set_a)
        prepare(2 * k + 1, *set_b)
        states = recur(2 * k, states)
        solve(2 * k + 1, *set_b)
        prepare(jnp.minimum(2 * k + 2, n_steps - 1), *set_a)
        return states

    zero_state = jnp.zeros((LANES, LANES), F32)
    states = lax.fori_loop(0, n_steps // 2, phase, (zero_state, zero_state))
    recur(n_steps - 1, states)

    def finish(i, carry):
        base = pl.multiple_of(i * DN_ROWS, DN_ROWS)
        o = o_ref[pl.ds(base, DN_ROWS), :]
        ms = _group_sum64(o * o, ones_bd) * (1.0 / DN_HD)
        o_ref[pl.ds(base, DN_ROWS), :] = o * lax.rsqrt(ms + LN_EPS) * ng_ref[...] * _silu(z_ref[pl.ds(base, DN_ROWS), :])
        return carry

    lax.fori_loop(0, seq // DN_ROWS, finish, 0)


def _deltanet(proj3, dn_conv, prm, norm_g2):
    bsz, seq, _ = proj3.shape
    cq, ck, cv, cz, cg = (COL_DNQ // LANES, COL_DNK // LANES, COL_DNV // LANES, COL_DNZ // LANES, COL_GATE // LANES)
    col = lambda c0: pl.BlockSpec((None, seq, LANES), lambda b, hp: (b, 0, c0 + hp))
    cw = lambda c0: pl.BlockSpec((DN_CONV, LANES), lambda b, hp: (0, c0 + hp))
    n_chunks = seq // DN_CHUNK
    n_chain = 2 * min(DN_UNROLL, n_chunks // 2)
    prep_set = [pltpu.VMEM((n_chain, PAIR, PAIR), BF16), pltpu.VMEM((n_chain, PAIR, PAIR + 2 * LANES), F32),
                pltpu.VMEM((n_chain, PAIR, PAIR), BF16), pltpu.VMEM((n_chain, DN_CHUNK, LANES), F32),
                pltpu.VMEM((n_chain, DN_CHUNK, LANES), BF16), pltpu.VMEM((n_chain, SUBLANES, LANES), F32)]
    return pl.pallas_call(
        functools.partial(_dn_body, seq=seq),
        grid=(bsz, DN_HEADS // 2),
        in_specs=[col(cq), col(ck), col(cv), col(cz), col(cg), cw(0), cw(2), cw(4),
                  pl.BlockSpec((None, SUBLANES, LANES), lambda b, hp: (hp, 0, 0)),
                  pl.BlockSpec((1, LANES), lambda b, hp: (0, 0))],
        out_specs=pl.BlockSpec((None, seq, LANES), lambda b, hp: (b, 0, hp)),
        out_shape=jax.ShapeDtypeStruct((bsz, seq, DN_HEADS * DN_HD), F32),
        scratch_shapes=[pltpu.VMEM((seq + 2 * DN_PAD, LANES), F32),
                        pltpu.VMEM((seq, LANES), F32), pltpu.VMEM((seq, LANES), F32), pltpu.VMEM((seq, LANES), F32),
                        pltpu.VMEM((seq, LANES), F32),
                        pltpu.VMEM((2, n_chunks, SUBLANES, LANES), F32),
                        pltpu.VMEM((2, n_chunks, LANES, LANES), BF16),
                        pltpu.VMEM((2, n_chunks, LANES, LANES), F32),
                        pltpu.VMEM((2, seq, LANES), BF16),
                        pltpu.VMEM((2, seq, LANES), F32)] + prep_set + prep_set,
        name="deltanet",
        compiler_params=_cparams(("parallel", "parallel")),
    )(proj3, proj3, proj3, proj3, proj3, dn_conv, dn_conv, dn_conv, prm, norm_g2)


def _outproj_body(yc_ref, yd_ref, yn_ref, x_ref, w_ref, g_ref, b_ref, o_ref):
    mix = jnp.concatenate([yc_ref[...], yd_ref[...], yn_ref[...]], axis=-1).astype(BF16)
    h = _dot(mix, w_ref[...])
    o_ref[...] = _layer_norm(DEEPNORM_ALPHA * x_ref[...] + h, g_ref[...], b_ref[...])


def _outproj_ln(yc, yd, yn, x2, w, g, b):
    t, d = x2.shape
    tm = min(512, t)
    row = lambda n: pl.BlockSpec((tm, n), lambda i: (i, 0))
    full = lambda shape: pl.BlockSpec(shape, lambda i: (0,) * len(shape))
    return pl.pallas_call(
        _outproj_body,
        grid=(t // tm,),
        in_specs=[row(yc.shape[1]), row(yd.shape[1]), row(yn.shape[1]), row(d), full(w.shape), full((1, d)), full((1, d))],
        out_specs=row(d),
        out_shape=jax.ShapeDtypeStruct((t, d), F32),
        name="outproj_ln",
        compiler_params=_cparams(("parallel",)),
    )(yc, yd, yn, x2, w, g, b)


def _ffn_body(x_ref, w1_ref, w3_ref, w2_ref, g_ref, b_ref, o_ref):
    x = x_ref[...]
    xb = x.astype(BF16)
    h = _silu(_dot(xb, w1_ref[...])) * _dot(xb, w3_ref[...])
    f = _dot(h.astype(BF16), w2_ref[...])
    o_ref[...] = _layer_norm(DEEPNORM_ALPHA * x + f, g_ref[...], b_ref[...])


def _ffn_ln(x2, w1, w3, w2, g, b):
    t, d = x2.shape
    f = w1.shape[1]
    tm = min(512, t)
    row = pl.BlockSpec((tm, d), lambda i: (i, 0))
    once = lambda shape: pl.BlockSpec(shape, lambda i: (0,) * len(shape), pipeline_mode=pl.Buffered(1))
    return pl.pallas_call(
        _ffn_body,
        grid=(t // tm,),
        in_specs=[row, once((d, f)), once((d, f)), once((f, d)), once((1, d)), once((1, d))],
        out_specs=row,
        out_shape=jax.ShapeDtypeStruct((t, d), F32),
        name="ffn_ln",
        compiler_params=_cparams(("parallel",)),
    )(x2, w1, w3, w2, g, b)


MOE_TB = 512
MOE_RUN = SUBLANES
MOE_BUF = 2 * MOE_TB + N_EXPERTS * MOE_RUN
META_P1, META_P2, META_G1, META_G2 = range(4)


def _moe_slots(t):
    n_blocks = t // min(MOE_TB, t)
    raw = 2 * t + n_blocks * N_EXPERTS * (MOE_RUN - 1) + N_EXPERTS * (MOE_TM - 1)
    return ((raw + MOE_TM - 1) // MOE_TM) * MOE_TM


def _router_body(x_ref, rw_ref, meta_ref, cnt_ref):
    tm = x_ref.shape[0]
    xh, xl = _split_bf16(x_ref[...])
    wh, wl = _split_bf16(rw_ref[...])
    logits = _dot(xh, wh) + _dot(xl, wh) + _dot(xh, wl)
    lane = lax.broadcasted_iota(jnp.int32, logits.shape, 1)
    logits = jnp.where(lane < N_EXPERTS, logits, -jnp.inf)
    m1 = jnp.max(logits, axis=-1, keepdims=True)
    e1 = jnp.min(jnp.where(logits == m1, lane, LANES), axis=-1, keepdims=True)
    rest = jnp.where(lane == e1, -jnp.inf, logits)
    m2 = jnp.max(rest, axis=-1, keepdims=True)
    e2 = jnp.min(jnp.where(rest == m2, lane, LANES), axis=-1, keepdims=True)
    t = jnp.exp(m2 - m1)
    g1 = 1.0 / (1.0 + t)
    g2 = t / (1.0 + t)

    sel = jnp.where((lane == e1) | (lane == e2), 1.0, 0.0)
    r = lax.broadcasted_iota(jnp.int32, (tm, tm), 0)
    c = lax.broadcasted_iota(jnp.int32, (tm, tm), 1)
    strict_lower = jnp.where(r > c, 1.0, 0.0).astype(BF16)
    rank = _dot(strict_lower, sel.astype(BF16))
    cnt = jnp.sum(sel, axis=0, keepdims=True)
    run = jnp.floor((cnt + (MOE_RUN - 1.0)) * (1.0 / MOE_RUN)) * MOE_RUN
    lane1 = lane[0:1]
    start = run
    sh = 1
    while sh < N_EXPERTS:
        start = start + jnp.where(lane1 >= sh, pltpu.roll(start, sh, 1), 0.0)
        sh *= 2
    pos = rank + (start - run)
    p1 = jnp.sum(jnp.where(lane == e1, pos, 0.0), axis=-1, keepdims=True)
    p2 = jnp.sum(jnp.where(lane == e2, pos, 0.0), axis=-1, keepdims=True)
    cnt_ref[...] = jnp.broadcast_to(cnt, cnt_ref.shape)

    meta = jnp.where(lane == META_P1, p1, 0.0)
    meta = jnp.where(lane == META_P2, p2, meta)
    meta = jnp.where(lane == META_G1, g1, meta)
    meta = jnp.where(lane == META_G2, g2, meta)
    meta_ref[...] = meta


def _router(x2, rw_pad):
    t, d = x2.shape
    tm = min(MOE_TB, t)
    return pl.pallas_call(
        _router_body,
        grid=(t // tm,),
        in_specs=[pl.BlockSpec((tm, d), lambda i: (i, 0)), pl.BlockSpec((d, LANES), lambda i: (0, 0))],
        out_specs=[pl.BlockSpec((tm, LANES), lambda i: (i, 0)),
                   pl.BlockSpec((None, SUBLANES, LANES), lambda i: (i, 0, 0))],
        out_shape=[jax.ShapeDtypeStruct((t, LANES), F32),
                   jax.ShapeDtypeStruct((t // tm, SUBLANES, LANES), F32)],
        name="router",
        compiler_params=_cparams(("parallel",)),
    )(x2, rw_pad)


def _for_each_run_piece(tab_ref, n_runs, blk, fn):
    for e in range(N_EXPERTS):
        k = blk * N_EXPERTS + e
        lo, run, dst = tab_ref[k], tab_ref[n_runs + k], tab_ref[2 * n_runs + k]
        bit = MOE_TB
        while bit >= MOE_RUN:
            done = run & ~(2 * bit - 1)

            @pl.when((run & bit) != 0)
            def _():
                fn(pl.multiple_of(lo + done, MOE_RUN), pl.multiple_of(dst + done, MOE_RUN), bit)

            bit //= 2


def _dispatch_body(tab_ref, x_ref, meta_ref, xs_ref, buf, sem, *, n_runs):
    blk = pl.program_id(0)
    last = pl.num_programs(0) - 1
    cur = blk % 2
    tb = x_ref.shape[0]

    def copy(s, buf_row, sorted_row, n):
        return pltpu.make_async_copy(buf.at[s, pl.ds(buf_row, n), :], xs_ref.at[pl.ds(sorted_row, n), :], sem.at[s])

    @pl.when(blk == 0)
    def _():
        buf[0] = jnp.zeros((MOE_BUF, buf.shape[2]), F32)

        def tail(e):
            row = pl.multiple_of(tab_ref[3 * n_runs + e], MOE_TM)
            return pltpu.make_async_copy(buf.at[0, pl.ds(0, MOE_TM), :], xs_ref.at[pl.ds(row, MOE_TM), :], sem.at[0])

        for e in range(N_EXPERTS):
            pl.when(tab_ref[3 * n_runs + e] >= 0)(lambda e=e: tail(e).start())
        for e in range(N_EXPERTS):
            pl.when(tab_ref[3 * n_runs + e] >= 0)(lambda e=e: tail(e).wait())

        def unused_tile(j, carry):
            row = pl.multiple_of(j * MOE_TM, MOE_TM)
            c = pltpu.make_async_copy(buf.at[0, pl.ds(0, MOE_TM), :], xs_ref.at[pl.ds(row, MOE_TM), :], sem.at[0])
            c.start()
            c.wait()
            return carry

        lax.fori_loop(tab_ref[3 * n_runs + N_EXPERTS], xs_ref.shape[0] // MOE_TM, unused_tile, 0)

    meta_t = meta_ref[...].T
    p1 = meta_t[META_P1:META_P1 + 1, :]
    p2 = meta_t[META_P2:META_P2 + 1, :]
    slot = lax.broadcasted_iota(jnp.int32, (MOE_BUF, tb), 0).astype(F32)
    onehot = jnp.where((slot == p1) | (slot == p2), 1.0, 0.0).astype(BF16)
    buf[cur] = _dot(onehot, x_ref[...].astype(BF16))

    _for_each_run_piece(tab_ref, n_runs, blk, lambda *a: copy(cur, *a).start())

    @pl.when(blk > 0)
    def _():
        _for_each_run_piece(tab_ref, n_runs, blk - 1, lambda *a: copy(1 - cur, *a).wait())

    @pl.when(blk == last)
    def _():
        _for_each_run_piece(tab_ref, n_runs, blk, lambda *a: copy(cur, *a).wait())


def _dispatch(tab, x2, meta, n_slots):
    t, d = x2.shape
    tb = min(MOE_TB, t)
    grid_spec = pltpu.PrefetchScalarGridSpec(
        num_scalar_prefetch=1,
        grid=(t // tb,),
        in_specs=[pl.BlockSpec((tb, d), lambda i, tab: (i, 0)),
                  pl.BlockSpec((tb, LANES), lambda i, tab: (i, 0))],
        out_specs=pl.BlockSpec(memory_space=pl.ANY),
        scratch_shapes=[pltpu.VMEM((2, MOE_BUF, d), F32), pltpu.SemaphoreType.DMA((2,))],
    )
    return pl.pallas_call(
        functools.partial(_dispatch_body, n_runs=(t // tb) * N_EXPERTS),
        grid_spec=grid_spec,
        out_shape=jax.ShapeDtypeStruct((n_slots, d), F32),
        name="moe_dispatch",
        compiler_params=pltpu.CompilerParams(dimension_semantics=("arbitrary",), vmem_limit_bytes=VMEM_LIMIT,
                                             has_side_effects=True),
    )(tab, x2, meta)


def _experts_body(te_ref, nu_ref, xs_ref, w1_ref, w3_ref, w2_ref, ys_ref, xb_scr, acc_scr):
    i = pl.program_id(0)
    f = pl.program_id(1)
    used = i < nu_ref[0]

    @pl.when(used & (f == 0))
    def _():
        xb_scr[...] = xs_ref[...].astype(BF16)

    @pl.when(used)
    def _():
        xb = xb_scr[...]
        h = _silu(_dot(xb, w1_ref[...])) * _dot(xb, w3_ref[...])
        part = _dot(h.astype(BF16), w2_ref[...])

        @pl.when(f == 0)
        def _():
            acc_scr[...] = part

        @pl.when(f > 0)
        def _():
            acc_scr[...] = acc_scr[...] + part

    @pl.when(f == pl.num_programs(1) - 1)
    def _():
        @pl.when(used)
        def _():
            ys_ref[...] = acc_scr[...]

        @pl.when(jnp.logical_not(used))
        def _():
            ys_ref[...] = jnp.zeros_like(ys_ref)


def _experts(tile_expert, n_used, xs, w1, w3, w2):
    n_slots, d = xs.shape
    n_tiles = n_slots // MOE_TM
    nf = w1.shape[2] // MOE_TF

    def fidx(i, f, te, nu):
        return jnp.where(i < nu[0], f, nf - 1)

    grid_spec = pltpu.PrefetchScalarGridSpec(
        num_scalar_prefetch=2,
        grid=(n_tiles, nf),
        in_specs=[pl.BlockSpec((MOE_TM, d), lambda i, f, te, nu: (jnp.minimum(i, nu[0] - 1), 0)),
                  pl.BlockSpec((None, d, MOE_TF), lambda i, f, te, nu: (te[i], 0, fidx(i, f, te, nu))),
                  pl.BlockSpec((None, d, MOE_TF), lambda i, f, te, nu: (te[i], 0, fidx(i, f, te, nu))),
                  pl.BlockSpec((None, MOE_TF, d), lambda i, f, te, nu: (te[i], fidx(i, f, te, nu), 0))],
        out_specs=pl.BlockSpec((MOE_TM, d), lambda i, f, te, nu: (i, 0)),
        scratch_shapes=[pltpu.VMEM((MOE_TM, d), BF16), pltpu.VMEM((MOE_TM, d), F32)],
    )
    return pl.pallas_call(
        _experts_body,
        grid_spec=grid_spec,
        out_shape=jax.ShapeDtypeStruct((n_slots, d), F32),
        name="moe_experts",
        compiler_params=_cparams(("arbitrary", "arbitrary")),
    )(tile_expert, n_used, xs, w1, w3, w2)


def _split_bf16(v):
    hi = v.astype(BF16)
    return hi, (v - hi.astype(F32)).astype(BF16)


def _combine_body(tab_ref, x_ref, meta_ref, ys_ref, g_ref, b_ref, o_ref, ybuf, sem, *, n_runs):
    blk = pl.program_id(0)
    last = pl.num_programs(0) - 1
    cur = blk % 2
    tb = x_ref.shape[0]

    def copy(s, buf_row, sorted_row, n):
        return pltpu.make_async_copy(ys_ref.at[pl.ds(sorted_row, n), :], ybuf.at[s, pl.ds(buf_row, n), :], sem.at[s])

    @pl.when(blk == 0)
    def _():
        ybuf[...] = jnp.zeros_like(ybuf)
        _for_each_run_piece(tab_ref, n_runs, blk, lambda *a: copy(cur, *a).start())

    @pl.when(blk < last)
    def _():
        _for_each_run_piece(tab_ref, n_runs, blk + 1, lambda *a: copy(1 - cur, *a).start())

    _for_each_run_piece(tab_ref, n_runs, blk, lambda *a: copy(cur, *a).wait())

    meta = meta_ref[...]
    p1 = meta[:, META_P1:META_P1 + 1]
    p2 = meta[:, META_P2:META_P2 + 1]
    g1 = meta[:, META_G1:META_G1 + 1]
    g2 = meta[:, META_G2:META_G2 + 1]
    slot = lax.broadcasted_iota(jnp.int32, (tb, MOE_BUF), 1).astype(F32)
    weights = jnp.where(slot == p1, g1, 0.0) + jnp.where(slot == p2, g2, 0.0)
    wh, wl = _split_bf16(weights)
    yh, yl = _split_bf16(ybuf[cur])
    f = _dot(wh, yh) + _dot(wh, yl) + _dot(wl, yh)
    o_ref[...] = _layer_norm(DEEPNORM_ALPHA * x_ref[...] + f, g_ref[...], b_ref[...])


def _combine_ln(tab, x2, meta, ys, g, b):
    t, d = x2.shape
    tb = min(MOE_TB, t)
    grid_spec = pltpu.PrefetchScalarGridSpec(
        num_scalar_prefetch=1,
        grid=(t // tb,),
        in_specs=[pl.BlockSpec((tb, d), lambda i, tab: (i, 0)),
                  pl.BlockSpec((tb, LANES), lambda i, tab: (i, 0)),
                  pl.BlockSpec(memory_space=pl.ANY),
                  pl.BlockSpec((1, d), lambda i, tab: (0, 0)),
                  pl.BlockSpec((1, d), lambda i, tab: (0, 0))],
        out_specs=pl.BlockSpec((tb, d), lambda i, tab: (i, 0)),
        scratch_shapes=[pltpu.VMEM((2, MOE_BUF, d), F32), pltpu.SemaphoreType.DMA((2,))],
    )
    return pl.pallas_call(
        functools.partial(_combine_body, n_runs=(t // tb) * N_EXPERTS),
        grid_spec=grid_spec,
        out_shape=jax.ShapeDtypeStruct((t, d), F32),
        name="moe_combine_ln",
        compiler_params=_cparams(("arbitrary",)),
    )(tab, x2, meta, ys, g, b)


def _moe_ln(x2, rw_pad, w1, w3, w2, g, b):
    t, d = x2.shape
    meta, blk_cnt = _router(x2, rw_pad)
    cnt = blk_cnt[:, 0, :N_EXPERTS].astype(jnp.int32)
    run = ((cnt + MOE_RUN - 1) // MOE_RUN) * MOE_RUN
    lo = jnp.cumsum(run, axis=1) - run
    before = jnp.cumsum(run, axis=0) - run
    padded = ((jnp.sum(run, axis=0) + MOE_TM - 1) // MOE_TM) * MOE_TM
    ends = jnp.cumsum(padded)
    starts = ends - padded
    tails = jnp.where(padded > 0, ends - MOE_TM, -1)
    n_slots = _moe_slots(t)
    n_tiles = n_slots // MOE_TM
    n_used = (ends[-1] // MOE_TM).astype(jnp.int32)
    tab = jnp.concatenate([lo.reshape(-1), run.reshape(-1), (starts[None, :] + before).reshape(-1),
                           tails, n_used.reshape(1)]).astype(jnp.int32)
    tile_start = jnp.arange(n_tiles, dtype=jnp.int32) * MOE_TM
    tile_expert = jnp.sum((tile_start[:, None] >= ends[None, :]).astype(jnp.int32), axis=1)
    last_expert = jnp.sum((((n_used - 1) * MOE_TM) >= ends).astype(jnp.int32))
    tile_expert = jnp.where(jnp.arange(n_tiles) < n_used, tile_expert, last_expert).astype(jnp.int32)

    xs = _dispatch(tab, x2, meta, n_slots)
    ys = _experts(tile_expert, n_used.reshape(1), xs, w1, w3, w2)
    return _combine_ln(tab, x2, meta, ys, g, b)


CAST_ROWS = 512


def _cast_body(w_ref, o_ref):
    o_ref[...] = w_ref[...].astype(BF16)


def _expert_weights_bf16(w, layer):
    _, n_exp, rows, cols = w.shape
    return pl.pallas_call(
        _cast_body,
        grid=(n_exp, rows // CAST_ROWS),
        in_specs=[pl.BlockSpec((None, None, CAST_ROWS, cols), lambda e, r: (layer, e, r, 0))],
        out_specs=pl.BlockSpec((None, CAST_ROWS, cols), lambda e, r: (e, r, 0)),
        out_shape=jax.ShapeDtypeStruct((n_exp, rows, cols), BF16),
        name="expert_weight_cast",
        compiler_params=_cparams(("parallel", "parallel")),
    )(w)
def _rope_tables(seq):
    inv = ROPE_THETA ** (-jnp.arange(0, DIFF_HD, 2, dtype=F32) / DIFF_HD)
    ang = jnp.arange(seq, dtype=F32)[:, None] * inv[None, :]
    cos, sin = jnp.cos(ang), jnp.sin(ang)
    cos_t = jnp.tile(cos, (1, LANES // cos.shape[1]))
    sin_t = jnp.tile(jnp.concatenate([-sin, sin], axis=1), (1, LANES // (2 * sin.shape[1])))
    return cos_t, sin_t


def _gate_columns():
    cols = []
    for hp in range(DN_HEADS // 2):
        blk = [PROJ_RAW - 16 + d * DN_HEADS + 2 * hp + hl for d in range(2) for hl in range(2)]
        blk += [PROJ_RAW - 8 + d * DN_HEADS + 2 * hp + hl for d in range(2) for hl in range(2)]
        cols.append(blk)
    return cols


def _prep_w_in(w_in_l):
    parts = [w_in_l[:, :COL_GATE]]
    for blk in _gate_columns():
        parts.append(w_in_l[:, jnp.array(blk)])
        parts.append(jnp.zeros((w_in_l.shape[0], LANES - len(blk)), w_in_l.dtype))
    return jnp.concatenate(parts, axis=1).astype(BF16)


def _prep_dn_params(a_log_l, dt_bias_l):
    out = []
    for hp in range(DN_HEADS // 2):
        idx = [(d, 2 * hp + hl) for d in range(2) for hl in range(2)]
        a = jnp.stack([a_log_l[d, h] for d, h in idx])
        t = jnp.stack([dt_bias_l[d, h] for d, h in idx])
        blk = jnp.zeros((SUBLANES, LANES), F32)
        blk = blk.at[0, 4:8].set(a).at[1, 4:8].set(t)
        out.append(blk)
    return jnp.stack(out)


def kernel(x, w_in, w_o, ln1_g, ln1_b, ln2_g, ln2_b, conv_dw, conv_dw_b, conv_ln_g, conv_ln_b, conv_pw,
           diff_lambda, diff_subln_g, dn_conv, dn_a_log, dn_dt_bias, dn_norm_g,
           ffn_w1, ffn_w3, ffn_w2, router_w, moe_w1, moe_w3, moe_w2):
    bsz, seq, d = x.shape
    t = bsz * seq
    cos_t, sin_t = _rope_tables(seq)
    x2 = x.reshape(t, d)
    row = lambda v: v.reshape(1, -1)
    for layer in range(DEPTH):
        lambda_init = 0.8 - 0.6 * math.exp(-0.3 * layer)
        proj = _inproj(x2, _prep_w_in(w_in[layer]))
        proj3 = proj.reshape(bsz, seq, PROJ_PAD)
        y_conv = _conv_module(proj3, conv_dw[layer], row(conv_dw_b[layer]), row(conv_ln_g[layer]),
                              row(conv_ln_b[layer]), conv_pw[layer].astype(BF16))
        y_diff = _diff_attention(proj3, cos_t, sin_t, diff_lambda[layer], diff_subln_g[layer].reshape(-1, 1),
                                 lambda_init)
        y_dn = _deltanet(proj3, dn_conv[layer], _prep_dn_params(dn_a_log[layer], dn_dt_bias[layer]),
                         row(jnp.tile(dn_norm_g[layer], 2)))
        x2 = _outproj_ln(y_conv.reshape(t, -1), y_diff.reshape(t, -1), y_dn.reshape(t, -1), x2,
                         w_o[layer].astype(BF16), row(ln1_g[layer]), row(ln1_b[layer]))
        j = layer // 2
        if layer % 2 == 0:
            x2 = _ffn_ln(x2, ffn_w1[j].astype(BF16), ffn_w3[j].astype(BF16), ffn_w2[j].astype(BF16),
                         row(ln2_g[layer]), row(ln2_b[layer]))
        else:
            rw_pad = jnp.pad(router_w[j], ((0, 0), (0, LANES - N_EXPERTS)))
            x2 = _moe_ln(x2, rw_pad, _expert_weights_bf16(moe_w1, j), _expert_weights_bf16(moe_w3, j),
                         _expert_weights_bf16(moe_w2, j), row(ln2_g[layer]), row(ln2_b[layer]))
    return x2.reshape(bsz, seq, d)
```
